```python
import math
import jax, jax.numpy as jnp
from jax import lax
import numpy as np

D_MODEL = 1024
BATCH = 8
SEQ = 2048
DEPTH = 4

HEAD_DIM = 64
N_HEADS_FOX = 8
N_HEADS_DIFF = 4
DIFF_V_DIM = 2 * HEAD_DIM
N_HEADS_DIL = 16
DILATED_CONFIGS = ((128, 1), (512, 4), (2048, 16))
ROT_DIM = HEAD_DIM // 4
ROPE_THETA = 500000.0
Q_BLOCK = 128
W_BLOCK = 128
NORM_EPS = 1e-6
WIDTH_FOX = N_HEADS_FOX * HEAD_DIM
DIFF_QK_WIDTH = N_HEADS_DIFF * 2 * HEAD_DIM
WIDTH_DIFF = N_HEADS_DIFF * DIFF_V_DIM
WIDTH_DIL = N_HEADS_DIL * HEAD_DIM
EVEN_IN = 4 * WIDTH_FOX + N_HEADS_FOX + 2 * DIFF_QK_WIDTH + 2 * WIDTH_DIFF
ODD_IN = 4 * WIDTH_DIL

kernel_name = 'hybrid_fox_diff_dilated_block'


def rms_norm(x, w):
    xf = x.astype(jnp.float32)
    var = jnp.mean(xf * xf, axis=-1, keepdims=True)
    return (xf * lax.rsqrt(var + NORM_EPS) * w.astype(jnp.float32)).astype(x.dtype)


def split_heads(t, n_heads, dh):
    b, s, _ = t.shape
    return t.reshape(b, s, n_heads, dh).transpose(0, 2, 1, 3)


def merge_heads(t):
    b, h, s, dh = t.shape
    return t.transpose(0, 2, 1, 3).reshape(b, s, h * dh)


def partial_rope(t, positions):
    inv_freq = 1.0 / (ROPE_THETA ** (jnp.arange(0, ROT_DIM, 2, dtype=jnp.float32) / ROT_DIM))
    ang = positions.astype(jnp.float32)[:, None, :, None] * inv_freq
    cos, sin = jnp.cos(ang), jnp.sin(ang)
    tr = t[..., :ROT_DIM].astype(jnp.float32)
    t1, t2 = tr[..., :ROT_DIM // 2], tr[..., ROT_DIM // 2:]
    rot = jnp.concatenate([t1 * cos - t2 * sin, t2 * cos + t1 * sin], axis=-1).astype(t.dtype)
    return jnp.concatenate([rot, t[..., ROT_DIM:]], axis=-1)


def fox_attention(q, k, v, log_f):
    b, h, s, dh = q.shape
    F = jnp.cumsum(log_f.astype(jnp.float32), axis=-1)
    scale = dh ** -0.5
    kpos = jnp.arange(s)

    def block(i):
        start = i * Q_BLOCK
        qb = lax.dynamic_slice_in_dim(q, start, Q_BLOCK, axis=2)
        Fq = lax.dynamic_slice_in_dim(F, start, Q_BLOCK, axis=2)
        sc = jnp.einsum('bhqd,bhkd->bhqk', qb, k, preferred_element_type=jnp.float32) * scale
        sc = sc + (Fq[..., :, None] - F[..., None, :])
        causal = kpos[None, :] <= (start + jnp.arange(Q_BLOCK))[:, None]
        p = jax.nn.softmax(jnp.where(causal, sc, -jnp.inf), axis=-1)
        return jnp.einsum('bhqk,bhkd->bhqd', p.astype(v.dtype), v)

    o = lax.map(block, jnp.arange(s // Q_BLOCK))
    return o.transpose(1, 2, 0, 3, 4).reshape(b, h, s, v.shape[-1])


def diff_attention(q1, q2, k1, k2, v, lam):
    b, h, s, dh = q1.shape
    scale = dh ** -0.5
    kpos = jnp.arange(s)

    def block(i):
        start = i * Q_BLOCK
        qb1 = lax.dynamic_slice_in_dim(q1, start, Q_BLOCK, axis=2)
        qb2 = lax.dynamic_slice_in_dim(q2, start, Q_BLOCK, axis=2)
        causal = kpos[None, :] <= (start + jnp.arange(Q_BLOCK))[:, None]
        s1 = jnp.einsum('bhqd,bhkd->bhqk', qb1, k1, preferred_element_type=jnp.float32) * scale
        s2 = jnp.einsum('bhqd,bhkd->bhqk', qb2, k2, preferred_element_type=jnp.float32) * scale
        p1 = jax.nn.softmax(jnp.where(causal, s1, -jnp.inf), axis=-1)
        p2 = jax.nn.softmax(jnp.where(causal, s2, -jnp.inf), axis=-1)
        p = p1 - lam * p2
        return jnp.einsum('bhqk,bhkd->bhqd', p.astype(v.dtype), v)

    o = lax.map(block, jnp.arange(s // Q_BLOCK))
    return o.transpose(1, 2, 0, 3, 4).reshape(b, h, s, v.shape[-1])


def dilated_window_branch(q, k, v, dilation, n_back):
    b, h, s, dh = q.shape
    L = s // dilation
    Lp = -(-L // W_BLOCK) * W_BLOCK
    nb = Lp // W_BLOCK

    def to_blocks(t):
        t = t.reshape(b, h, L, dilation, t.shape[-1]).transpose(0, 1, 3, 2, 4)
        t = jnp.pad(t, ((0, 0), (0, 0), (0, 0), (0, Lp - L), (0, 0)))
        return t.reshape(b, h, dilation, nb, W_BLOCK, t.shape[-1])

    def band(t):
        prev = jnp.pad(t[:, :, :, :-1], ((0, 0), (0, 0), (0, 0), (1, 0), (0, 0), (0, 0)))
        return jnp.concatenate([prev, t], axis=4)

    qb = to_blocks(q)
    kband = band(to_blocks(k))
    vband = band(to_blocks(v))
    sc = jnp.einsum('bhrnqd,bhrnkd->bhrnqk', qb, kband, preferred_element_type=jnp.float32) * dh ** -0.5
    qi = jnp.arange(W_BLOCK)
    ki = jnp.arange(2 * W_BLOCK) - W_BLOCK
    dist = qi[:, None] - ki[None, :]
    in_window = (dist >= 0) & (dist <= n_back)
    key_exists = (jnp.arange(nb)[:, None, None] * W_BLOCK + ki[None, None, :]) >= 0
    mask = in_window[None] & key_exists
    sc = jnp.where(mask, sc, -jnp.inf)
    m = jnp.max(sc, axis=-1, keepdims=True)
    e = jnp.exp(sc - m)
    l = jnp.sum(e, axis=-1, keepdims=True)
    o = jnp.einsum('bhrnqk,bhrnkd->bhrnqd', (e / l).astype(v.dtype), vband)

    def from_blocks(t):
        t = t.reshape(b, h, dilation, Lp, t.shape[-1])[:, :, :, :L]
        return t.transpose(0, 1, 3, 2, 4).reshape(b, h, s, t.shape[-1])

    return from_blocks(o), from_blocks(m), from_blocks(l)


def dilated_mixture(q, k, v):
    branches = [dilated_window_branch(q, k, v, d, w // d) for (w, d) in DILATED_CONFIGS]
    big_m = branches[0][1]
    for _, m_i, _ in branches[1:]:
        big_m = jnp.maximum(big_m, m_i)
    num = 0.0
    den = 0.0
    for o_i, m_i, l_i in branches:
        w_i = l_i * jnp.exp(m_i - big_m)
        num = num + w_i * o_i.astype(jnp.float32)
        den = den + w_i
    return (num / den).astype(v.dtype)


def fox_diff_mixer(h, positions, w_in, b_forget, lam_q1, lam_k1, lam_q2, lam_k2, subln, w_out, layer_idx):
    b, s, _ = h.shape
    proj = jnp.einsum('bsd,de->bse', h, w_in)
    sizes = (WIDTH_FOX, WIDTH_FOX, WIDTH_FOX, N_HEADS_FOX, WIDTH_FOX,
             DIFF_QK_WIDTH, DIFF_QK_WIDTH, WIDTH_DIFF, WIDTH_DIFF)
    cuts = [int(v) for v in np.cumsum(sizes)[:-1]]
    qa, ka, va, fa, ga, qd, kd, vd, gd = jnp.split(proj, cuts, axis=-1)
    log_f = jax.nn.log_sigmoid((fa + b_forget).astype(jnp.float32)).transpose(0, 2, 1)
    oa = fox_attention(split_heads(qa, N_HEADS_FOX, HEAD_DIM), split_heads(ka, N_HEADS_FOX, HEAD_DIM),
                       split_heads(va, N_HEADS_FOX, HEAD_DIM), log_f)
    out_a = merge_heads(oa) * jax.nn.silu(ga)
    def sub_heads(t):
        return t.reshape(b, s, N_HEADS_DIFF, 2, HEAD_DIM).transpose(0, 2, 3, 1, 4)
    qd2, kd2 = sub_heads(qd), sub_heads(kd)
    q1, q2 = partial_rope(qd2[:, :, 0], positions), partial_rope(qd2[:, :, 1], positions)
    k1, k2 = partial_rope(kd2[:, :, 0], positions), partial_rope(kd2[:, :, 1], positions)
    lam_init = 0.8 - 0.6 * math.exp(-0.3 * layer_idx)
    lam = (jnp.exp(jnp.sum(lam_q1.astype(jnp.float32) * lam_k1.astype(jnp.float32)))
           - jnp.exp(jnp.sum(lam_q2.astype(jnp.float32) * lam_k2.astype(jnp.float32))) + lam_init)
    od = diff_attention(q1, q2, k1, k2, split_heads(vd, N_HEADS_DIFF, DIFF_V_DIM), lam)
    od = rms_norm(od, subln) * (1.0 - lam_init)
    out_b = merge_heads(od) * jax.nn.silu(gd)
    return jnp.einsum('bse,ed->bsd', jnp.concatenate([out_a, out_b], axis=-1), w_out)


def dilated_mixer(h, positions, w_in, w_out):
    proj = jnp.einsum('bsd,de->bse', h, w_in)
    qc, kc, vc, gc = jnp.split(proj, 4, axis=-1)
    q = partial_rope(split_heads(qc, N_HEADS_DIL, HEAD_DIM), positions)
    k = partial_rope(split_heads(kc, N_HEADS_DIL, HEAD_DIM), positions)
    v = split_heads(vc, N_HEADS_DIL, HEAD_DIM)
    oc = merge_heads(dilated_mixture(q, k, v)) * jax.nn.silu(gc)
    return jnp.einsum('bse,ed->bsd', oc, w_out)


def setup_inputs(seed: int = 0) -> dict:
    key = jax.random.key(seed)
    ks = jax.random.split(key, 18)
    n_even = (DEPTH + 1) // 2
    n_odd = DEPTH // 2

    def nrm(k, shape, scale):
        return jax.random.normal(k, shape, jnp.float32) * scale

    x = nrm(ks[0], (BATCH, SEQ, D_MODEL), 1.0)
    c = nrm(ks[1], (BATCH, D_MODEL), 1.0)
    offsets = jax.random.randint(ks[2], (BATCH, 1), 0, 4096, dtype=jnp.int32)
    positions = (offsets + jnp.arange(SEQ, dtype=jnp.int32)[None, :]).astype(jnp.int32)
    norm_pre = 1.0 + nrm(ks[3], (DEPTH, D_MODEL), 0.1)
    norm_post = 1.0 + nrm(ks[4], (DEPTH, D_MODEL), 0.1)
    ada_w = nrm(ks[5], (DEPTH, D_MODEL, 3 * D_MODEL), D_MODEL ** -0.5)
    ada_b = nrm(ks[6], (DEPTH, 3 * D_MODEL), 0.02)
    ev_w_in = nrm(ks[7], (n_even, D_MODEL, EVEN_IN), D_MODEL ** -0.5)
    ev_b_forget = 2.0 + nrm(ks[8], (n_even, N_HEADS_FOX), 0.5)
    ev_lambda_q1 = nrm(ks[9], (n_even, HEAD_DIM), 0.1)
    ev_lambda_k1 = nrm(ks[10], (n_even, HEAD_DIM), 0.1)
    ev_lambda_q2 = nrm(ks[11], (n_even, HEAD_DIM), 0.1)
    ev_lambda_k2 = nrm(ks[12], (n_even, HEAD_DIM), 0.1)
    ev_subln = 1.0 + nrm(ks[13], (n_even, DIFF_V_DIM), 0.1)
    ev_w_out = nrm(ks[14], (n_even, WIDTH_FOX + WIDTH_DIFF, D_MODEL), (WIDTH_FOX + WIDTH_DIFF) ** -0.5)
    od_w_in = nrm(ks[15], (n_odd, D_MODEL, ODD_IN), D_MODEL ** -0.5)
    od_w_out = nrm(ks[16], (n_odd, WIDTH_DIL, D_MODEL), WIDTH_DIL ** -0.5)
    return {'x': x, 'c': c, 'positions': positions, 'norm_pre': norm_pre, 'norm_post': norm_post,
            'ada_w': ada_w, 'ada_b': ada_b, 'ev_w_in': ev_w_in, 'ev_b_forget': ev_b_forget,
            'ev_lambda_q1': ev_lambda_q1, 'ev_lambda_k1': ev_lambda_k1,
            'ev_lambda_q2': ev_lambda_q2, 'ev_lambda_k2': ev_lambda_k2, 'ev_subln': ev_subln,
            'ev_w_out': ev_w_out, 'od_w_in': od_w_in, 'od_w_out': od_w_out}


def reference(x, c, positions, norm_pre, norm_post, ada_w, ada_b, ev_w_in, ev_b_forget,
              ev_lambda_q1, ev_lambda_k1, ev_lambda_q2, ev_lambda_k2, ev_subln, ev_w_out,
              od_w_in, od_w_out):
    cond = jax.nn.silu(c)
    for layer in range(DEPTH):
        mod = jnp.einsum('bd,de->be', cond, ada_w[layer]) + ada_b[layer]
        shift, scale, gate = jnp.split(mod, 3, axis=-1)
        h = rms_norm(x, norm_pre[layer]) * (1.0 + scale[:, None, :]) + shift[:, None, :]
        if layer % 2 == 0:
            i = layer // 2
            y = fox_diff_mixer(h, positions, ev_w_in[i], ev_b_forget[i], ev_lambda_q1[i], ev_lambda_k1[i],
                               ev_lambda_q2[i], ev_lambda_k2[i], ev_subln[i], ev_w_out[i], layer)
        else:
            j = layer // 2
            y = dilated_mixer(h, positions, od_w_in[j], od_w_out[j])
        x = x + gate[:, None, :] * rms_norm(y, norm_post[layer])
    return x
```

```python
import functools
import math

import numpy as np
import jax
import jax.numpy as jnp
from jax import lax
from jax.experimental import pallas as pl
from jax.experimental.pallas import tpu as pltpu

D_MODEL = 1024
HEAD_DIM = 64
N_HEADS_FOX = 8
ROT_DIM = HEAD_DIM // 4
ROPE_THETA = 500000.0
NORM_EPS = 1e-6
DILATED_CONFIGS = ((128, 1), (512, 4), (2048, 16))

LANES = 128
MXU_COLS = 256
N_SLABS = 32
N_UNITS = 8
TQ = 256
TK = 256
TM = 512
NEG = -1e30
VMEM_LIMIT = 48 * 1024 * 1024


def _cparams(n_axes):
    return pltpu.CompilerParams(
        dimension_semantics=("arbitrary",) * n_axes, vmem_limit_bytes=VMEM_LIMIT)


def _mod_kernel(c_ref, w_ref, b_ref, o_ref):
    c = c_ref[...]
    cond = c * jax.nn.sigmoid(c)
    o_ref[0, 0] = jnp.dot(cond, w_ref[0], preferred_element_type=jnp.float32) + b_ref[0, 0]


def _adaln_mod(c, ada_w, ada_b):
    depth = ada_w.shape[0]
    b = c.shape[0]
    out = pl.pallas_call(
        _mod_kernel,
        grid=(depth, 3),
        in_specs=[
            pl.BlockSpec((b, D_MODEL), lambda l, j: (0, 0)),
            pl.BlockSpec((1, D_MODEL, D_MODEL), lambda l, j: (l, 0, j)),
            pl.BlockSpec((1, 1, 1, D_MODEL), lambda l, j: (l, j, 0, 0)),
        ],
        out_specs=pl.BlockSpec((1, 1, b, D_MODEL), lambda l, j: (l, j, 0, 0)),
        out_shape=jax.ShapeDtypeStruct((depth, 3, b, D_MODEL), jnp.float32),
        compiler_params=_cparams(2),
        name="adaln_mod",
    )(c, ada_w, ada_b.reshape(depth, 3, 1, D_MODEL))
    return out.reshape(depth, 3, b, 1, D_MODEL)


def _rope_kernel(pos_ref, c_ref, s1_ref, s2_ref):
    pos = pos_ref[0].astype(jnp.float32)
    lane = lax.broadcasted_iota(jnp.int32, (1, LANES), 1)
    cc = lane % HEAD_DIM
    fidx = (cc % (ROT_DIM // 2)).astype(jnp.float32)
    theta = jnp.full((1, LANES), ROPE_THETA, jnp.float32)
    inv_freq = jnp.exp(-(fidx * (2.0 / ROT_DIM)) * jnp.log(theta))
    ang = pos * inv_freq
    cos = jnp.cos(ang)
    sin = jnp.sin(ang)
    c_ref[0] = jnp.where(cc < ROT_DIM, cos, 1.0)
    s1_ref[0] = jnp.where(cc < ROT_DIM // 2, -sin, 0.0)
    s2_ref[0] = jnp.where((cc >= ROT_DIM // 2) & (cc < ROT_DIM), sin, 0.0)


def _rope_tables(positions):
    b, s = positions.shape
    spec = pl.BlockSpec((1, s, LANES), lambda i: (i, 0, 0))
    shp = jax.ShapeDtypeStruct((b, s, LANES), jnp.float32)
    return pl.pallas_call(
        _rope_kernel,
        grid=(b,),
        in_specs=[pl.BlockSpec((1, s, 1), lambda i: (i, 0, 0))],
        out_specs=[spec, spec, spec],
        out_shape=[shp, shp, shp],
        compiler_params=_cparams(1),
        name="rope_tables",
    )(positions.reshape(b, s, 1))


def _inproj_kernel(*refs, rope_slabs, with_forget):
    if with_forget:
        (x_ref, shift_ref, scale_ref, npre_ref, w_ref, wf_ref, c_ref, s1_ref, s2_ref,
         out_ref, fa_ref) = refs
    else:
        (x_ref, shift_ref, scale_ref, npre_ref, w_ref, c_ref, s1_ref, s2_ref, out_ref) = refs
    x = x_ref[0]
    var = jnp.mean(x * x, axis=-1, keepdims=True)
    h = (x * lax.rsqrt(var + NORM_EPS) * npre_ref[...]) * (1.0 + scale_ref[...]) + shift_ref[...]
    hb = h.astype(jnp.bfloat16)
    cos = c_ref[0]
    s1 = s1_ref[0]
    s2 = s2_ref[0]
    for c in range(N_SLABS * LANES // MXU_COLS):
        r = jnp.dot(hb, w_ref[:, c * MXU_COLS:(c + 1) * MXU_COLS],
                    preferred_element_type=jnp.float32)
        for half in range(MXU_COLS // LANES):
            j = c * (MXU_COLS // LANES) + half
            y = r[:, half * LANES:(half + 1) * LANES]
            if j in rope_slabs:
                y = (y * cos + pltpu.roll(y, LANES - ROT_DIM // 2, 1) * s1
                     + pltpu.roll(y, ROT_DIM // 2, 1) * s2)
            out_ref[0, j] = y.astype(jnp.bfloat16)
    if with_forget:
        fa_ref[0] = jnp.dot(hb, wf_ref[...], preferred_element_type=jnp.float32)


def _in_projection(x, shift, scale, npre, w, wf, rope, rope_slabs):
    b, s, d = x.shape
    with_forget = wf is not None
    row = lambda bi, i: (bi, i, 0)
    const2 = lambda bi, i: (0, 0)
    in_specs = [
        pl.BlockSpec((1, TM, d), row),
        pl.BlockSpec((None, 1, d), lambda bi, i: (bi, 0, 0)),
        pl.BlockSpec((None, 1, d), lambda bi, i: (bi, 0, 0)),
        pl.BlockSpec((1, d), const2),
        pl.BlockSpec(w.shape, const2),
    ]
    args = [x, shift, scale, npre.reshape(1, d), w]
    if with_forget:
        in_specs.append(pl.BlockSpec(wf.shape, const2))
        args.append(wf)
    in_specs += [pl.BlockSpec((1, TM, LANES), row)] * 3
    args += list(rope)
    out_specs = [pl.BlockSpec((1, N_SLABS, TM, LANES), lambda bi, i: (bi, 0, i, 0))]
    out_shape = [jax.ShapeDtypeStruct((b, N_SLABS, s, LANES), jnp.bfloat16)]
    if with_forget:
        out_specs.append(pl.BlockSpec((1, TM, LANES), row))
        out_shape.append(jax.ShapeDtypeStruct((b, s, LANES), jnp.float32))
    res = pl.pallas_call(
        functools.partial(_inproj_kernel, rope_slabs=frozenset(rope_slabs),
                          with_forget=with_forget),
        grid=(b, s // TM),
        in_specs=in_specs,
        out_specs=out_specs,
        out_shape=out_shape,
        compiler_params=_cparams(2),
        name="in_projection",
    )(*args)
    return res if with_forget else (res[0], None)


F_BLOCK = 256


def _fgate_kernel(fa_ref, b_ref, g_ref):
    s = fa_ref.shape[1]
    z = fa_ref[0] + b_ref[...]
    logf = jnp.minimum(z, 0.0) - jnp.log(1.0 + jnp.exp(-jnp.abs(z)))
    r_i = lax.broadcasted_iota(jnp.int32, (F_BLOCK, F_BLOCK), 0)
    c_i = lax.broadcasted_iota(jnp.int32, (F_BLOCK, F_BLOCK), 1)
    tri = jnp.where(c_i <= r_i, 1.0, 0.0).astype(jnp.bfloat16)
    lane = lax.broadcasted_iota(jnp.int32, (1, LANES), 1)

    def split3(v):
        hi = v.astype(jnp.bfloat16)
        r1 = v - hi.astype(jnp.float32)
        mid = r1.astype(jnp.bfloat16)
        lo = (r1 - mid.astype(jnp.float32)).astype(jnp.bfloat16)
        return hi, mid, lo

    carry = jnp.zeros((1, LANES), jnp.float32)
    for i in range(s // F_BLOCK):
        blk = logf[i * F_BLOCK:(i + 1) * F_BLOCK]
        parts = jnp.concatenate(split3(blk), axis=1)
        cs3 = jnp.dot(tri, parts, preferred_element_type=jnp.float32)
        f = (cs3[:, :LANES] + cs3[:, LANES:2 * LANES]) + cs3[:, 2 * LANES:] + carry
        carry = f[F_BLOCK - 1:F_BLOCK, :]
        hi, mid, lo = split3(f)
        packed = (jnp.where(lane < N_HEADS_FOX, hi.astype(jnp.float32), 0.0)
                  + pltpu.roll(jnp.where(lane < N_HEADS_FOX, mid.astype(jnp.float32), 0.0),
                               N_HEADS_FOX, 1)
                  + pltpu.roll(jnp.where(lane < N_HEADS_FOX, lo.astype(jnp.float32), 0.0),
                               2 * N_HEADS_FOX, 1))
        g_ref[0, i * F_BLOCK:(i + 1) * F_BLOCK, :] = packed.astype(jnp.bfloat16)


def _forget_prefix(fa, b_forget):
    b, s, _ = fa.shape
    bpad = jnp.zeros((1, LANES), jnp.float32).at[0, :N_HEADS_FOX].set(b_forget)
    return pl.pallas_call(
        _fgate_kernel,
        grid=(b,),
        in_specs=[pl.BlockSpec((1, s, LANES), lambda i: (i, 0, 0)),
                  pl.BlockSpec((1, LANES), lambda i: (0, 0))],
        out_specs=pl.BlockSpec((1, s, LANES), lambda i: (i, 0, 0)),
        out_shape=jax.ShapeDtypeStruct((b, s, LANES), jnp.bfloat16),
        compiler_params=_cparams(1),
        name="forget_prefix",
    )(fa, bpad)


def _bias_tables(mode):
    r = np.arange(TQ)[:, None]
    c = np.arange(TK)[None, :]
    tabs = []
    for off in range(4):
        delta = off * TQ + r - c
        if mode == "even":
            mult = (delta >= 0).astype(np.float64)
        else:
            mult = np.zeros((TQ, TK))
            for (w, d) in DILATED_CONFIGS:
                ok = (delta >= 0) & (delta % d == 0) & (delta <= (w // d) * d)
                if off == 3:
                    ok = ok & ((w // d) * d >= 2048 - 1)
                mult += ok
        with np.errstate(divide="ignore"):
            tabs.append(np.where(mult > 0, np.log(np.maximum(mult, 1e-30)), NEG))
    return jnp.asarray(np.stack(tabs), jnp.float32)


def _attn_kernel(*refs, mode, lam_init):
    if mode == "even":
        (q_ref, k_ref, v_ref, g_ref, aug_ref, tab_ref, par_ref, o_ref,
         kcat, m1, l1, a1, m2, l2, a2) = refs
    else:
        (q_ref, k_ref, v_ref, g_ref, tab_ref, o_ref, m1, l1, a1, m2, l2, a2) = refs
    s_len = q_ref.shape[2]
    u = pl.program_id(1)
    lane = lax.broadcasted_iota(jnp.int32, (1, LANES), 1)
    lo = lane < HEAD_DIM
    nh = N_HEADS_FOX

    if mode == "even":
        is_fox = u < N_UNITS // 2
        fox_f = jnp.where(is_fox, 1.0, 0.0)
        g_all = aug_ref[0].astype(jnp.float32)
        kaug = jnp.where(lane < 3 * nh, -g_all, jnp.where(lane < 6 * nh, 1.0, 0.0)) * fox_f
        kcat[:, :LANES] = k_ref[0, 0]
        kcat[:, LANES:] = kaug.astype(jnp.bfloat16)

    def q_block(qi, carry):
        q0 = pl.multiple_of(qi * TQ, TQ)
        qs = q_ref[0, 0, pl.ds(q0, TQ), :].astype(jnp.float32) * (HEAD_DIM ** -0.5)
        lhs = [jnp.where(lo, qs, 0.0).astype(jnp.bfloat16),
               jnp.where(lo, 0.0, qs).astype(jnp.bfloat16)]
        if mode == "even":
            gq = aug_ref[0, pl.ds(q0, TQ), :].astype(jnp.float32)
            gr = pltpu.roll(gq, 3 * nh, 1)
            for i in range(2):
                h = 2 * u + i
                sel_k = (lane == h) | (lane == h + nh) | (lane == h + 2 * nh)
                sel_q = (lane == h + 3 * nh) | (lane == h + 4 * nh) | (lane == h + 5 * nh)
                qa = jnp.where(sel_k, 1.0, jnp.where(sel_q, gr, 0.0)) * fox_f
                lhs[i] = jnp.concatenate([lhs[i], qa.astype(jnp.bfloat16)], axis=1)
        state = ((m1, l1, a1), (m2, l2, a2))
        for (m_r, l_r, a_r) in state:
            m_r[...] = jnp.full((TQ, LANES), NEG, jnp.float32)
            l_r[...] = jnp.zeros((TQ, LANES), jnp.float32)
            a_r[...] = jnp.zeros((TQ, LANES), jnp.float32)

        def kv_block(kj, c2):
            k0 = pl.multiple_of(kj * TK, TK)
            if mode == "even":
                kb = kcat[pl.ds(k0, TK), :]
            else:
                kb = k_ref[0, 0, pl.ds(k0, TK), :]
            vb = v_ref[0, 0, pl.ds(k0, TK), :]
            bias = tab_ref[jnp.minimum(qi - kj, 3)]
            for i in range(2):
                m_r, l_r, a_r = state[i]
                s = lax.dot_general(lhs[i], kb, (((1,), (1,)), ((), ())),
                                    preferred_element_type=jnp.float32) + bias
                m_prev = m_r[...]
                m_next = jnp.maximum(m_prev, jnp.max(s, axis=1, keepdims=True))
                p = jnp.exp(s - jnp.concatenate([m_next] * (TK // LANES), axis=1))
                alpha = jnp.exp(m_prev - m_next)
                l_r[...] = alpha * l_r[...] + jnp.sum(p, axis=1, keepdims=True)
                a_r[...] = alpha * a_r[...] + jnp.dot(p.astype(jnp.bfloat16), vb,
                                                      preferred_element_type=jnp.float32)
                m_r[...] = m_next
            return c2

        lax.fori_loop(0, qi + 1, kv_block, 0)

        o1 = a1[...] / l1[...]
        o2 = a2[...] / l2[...]
        g = g_ref[0, 0, pl.ds(q0, TQ), :].astype(jnp.float32)
        gate = g * jax.nn.sigmoid(g)
        out = jnp.where(lo, o1, o2)
        if mode == "even":
            par = par_ref[...]
            lam = (jnp.exp(jnp.sum(par[0:1] * par[1:2], axis=1, keepdims=True))
                   - jnp.exp(jnp.sum(par[2:3] * par[3:4], axis=1, keepdims=True)) + lam_init)
            d = o1 - lam * o2
            var = jnp.mean(d * d, axis=-1, keepdims=True)
            dn = (d * lax.rsqrt(var + NORM_EPS) * par[4:5]) * (1.0 - lam_init)
            out = jnp.where(is_fox, out, dn)
        o_ref[0, 0, pl.ds(q0, TQ), :] = (out * gate).astype(jnp.bfloat16)
        return carry

    lax.fori_loop(0, s_len // TQ, q_block, 0)


def _attention(proj, mode, aug=None, params=None, lam_init=0.0):
    b, _, s, _ = proj.shape
    tabs = _bias_tables(mode)

    def slab(kind):
        return pl.BlockSpec((1, 1, s, LANES), lambda bi, u: (bi, kind * N_UNITS + u, 0, 0))

    in_specs = [slab(0), slab(1), slab(2), slab(3)]
    args = [proj, proj, proj, proj]
    scratch = []
    if mode == "even":
        in_specs.append(pl.BlockSpec((1, s, LANES), lambda bi, u: (bi, 0, 0)))
        args.append(aug)
        scratch.append(pltpu.VMEM((s, 2 * LANES), jnp.bfloat16))
    in_specs.append(pl.BlockSpec(tabs.shape, lambda bi, u: (0, 0, 0)))
    args.append(tabs)
    if mode == "even":
        in_specs.append(pl.BlockSpec(params.shape, lambda bi, u: (0, 0)))
        args.append(params)
    scratch += [pltpu.VMEM((TQ, LANES), jnp.float32)] * 6
    return pl.pallas_call(
        functools.partial(_attn_kernel, mode=mode, lam_init=lam_init),
        grid=(b, N_UNITS),
        in_specs=in_specs,
        out_specs=pl.BlockSpec((1, 1, s, LANES), lambda bi, u: (bi, u, 0, 0)),
        out_shape=jax.ShapeDtypeStruct((b, N_UNITS, s, LANES), jnp.bfloat16),
        scratch_shapes=scratch,
        compiler_params=_cparams(2),
        name="attention_" + mode,
    )(*args)


def _outproj_kernel(o_ref, w_ref, x_ref, gate_ref, npost_ref, out_ref):
    cat = jnp.concatenate([o_ref[0, j] for j in range(N_UNITS)], axis=1)
    y = jnp.dot(cat, w_ref[...], preferred_element_type=jnp.float32)
    var = jnp.mean(y * y, axis=-1, keepdims=True)
    out_ref[0] = x_ref[0] + gate_ref[...] * (y * lax.rsqrt(var + NORM_EPS) * npost_ref[...])


def _out_projection(o, w, x, gate, npost):
    b, s, d = x.shape
    return pl.pallas_call(
        _outproj_kernel,
        grid=(b, s // TM),
        in_specs=[
            pl.BlockSpec((1, N_UNITS, TM, LANES), lambda bi, i: (bi, 0, i, 0)),
            pl.BlockSpec(w.shape, lambda bi, i: (0, 0)),
            pl.BlockSpec((1, TM, d), lambda bi, i: (bi, i, 0)),
            pl.BlockSpec((None, 1, d), lambda bi, i: (bi, 0, 0)),
            pl.BlockSpec((1, d), lambda bi, i: (0, 0)),
        ],
        out_specs=pl.BlockSpec((1, TM, d), lambda bi, i: (bi, i, 0)),
        out_shape=jax.ShapeDtypeStruct((b, s, d), jnp.float32),
        compiler_params=_cparams(2),
        name="out_projection",
    )(o, w, x, gate, npost.reshape(1, d))


def _even_weight_layout(w_in):
    wf_, wd = N_HEADS_FOX * HEAD_DIM, 4 * 2 * HEAD_DIM
    cuts = np.cumsum([wf_, wf_, wf_, N_HEADS_FOX, wf_, wd, wd, wd, wd])
    qa, ka, va, fa, ga, qd, kd, vd, gd = jnp.split(w_in, [int(v) for v in cuts[:-1]], axis=1)
    main = jnp.concatenate([qa, qd, ka, kd, va, vd, ga, gd], axis=1).astype(jnp.bfloat16)
    wf = jnp.pad(fa, ((0, 0), (0, LANES - N_HEADS_FOX))).astype(jnp.bfloat16)
    return main, wf


def kernel(x, c, positions, norm_pre, norm_post, ada_w, ada_b, ev_w_in, ev_b_forget,
           ev_lambda_q1, ev_lambda_k1, ev_lambda_q2, ev_lambda_k2, ev_subln, ev_w_out,
           od_w_in, od_w_out):
    depth = ada_w.shape[0]
    mod = _adaln_mod(c, ada_w, ada_b)
    rope = _rope_tables(positions)
    half = N_UNITS // 2
    even_rope = set(range(half, N_UNITS)) | set(range(N_UNITS + half, 2 * N_UNITS))
    odd_rope = set(range(2 * N_UNITS))
    for layer in range(depth):
        shift, scale, gate = mod[layer, 0], mod[layer, 1], mod[layer, 2]
        if layer % 2 == 0:
            i = layer // 2
            w_main, wf = _even_weight_layout(ev_w_in[i])
            proj, fa = _in_projection(x, shift, scale, norm_pre[layer], w_main, wf, rope, even_rope)
            aug = _forget_prefix(fa, ev_b_forget[i])
            params = jnp.zeros((8, LANES), jnp.float32)
            params = params.at[0, :HEAD_DIM].set(ev_lambda_q1[i]).at[1, :HEAD_DIM].set(ev_lambda_k1[i])
            params = params.at[2, :HEAD_DIM].set(ev_lambda_q2[i]).at[3, :HEAD_DIM].set(ev_lambda_k2[i])
            params = params.at[4, :].set(ev_subln[i])
            lam_init = 0.8 - 0.6 * math.exp(-0.3 * layer)
            o = _attention(proj, "even", aug=aug, params=params, lam_init=lam_init)
            w_out = ev_w_out[i].astype(jnp.bfloat16)
        else:
            j = layer // 2
            proj, _ = _in_projection(x, shift, scale, norm_pre[layer],
                                     od_w_in[j].astype(jnp.bfloat16), None, rope, odd_rope)
            o = _attention(proj, "odd")
            w_out = od_w_out[j].astype(jnp.bfloat16)
        x = _out_projection(o, w_out, x, gate, norm_post[layer])
    return x
```

```python
import functools
import math

import numpy as np
import jax
import jax.numpy as jnp
from jax import lax
from jax.experimental import pallas as pl
from jax.experimental.pallas import tpu as pltpu

D_MODEL = 1024
HEAD_DIM = 64
N_HEADS_FOX = 8
ROT_DIM = HEAD_DIM // 4
ROPE_THETA = 500000.0
NORM_EPS = 1e-6
DILATED_CONFIGS = ((128, 1), (512, 4), (2048, 16))

LANES = 128
MXU_COLS = 256
N_SLABS = 32
N_UNITS = 8
TQ = 512
TK = 512
TM = 512
NEG = -1e30
VMEM_LIMIT = 48 * 1024 * 1024


def _cparams(n_axes):
    return pltpu.CompilerParams(
        dimension_semantics=("arbitrary",) * n_axes, vmem_limit_bytes=VMEM_LIMIT)


def _mod_kernel(c_ref, w_ref, b_ref, o_ref):
    c = c_ref[...]
    cond = c * jax.nn.sigmoid(c)
    o_ref[0, 0] = jnp.dot(cond, w_ref[0], preferred_element_type=jnp.float32) + b_ref[0, 0]


def _adaln_mod(c, ada_w, ada_b):
    depth = ada_w.shape[0]
    b = c.shape[0]
    out = pl.pallas_call(
        _mod_kernel,
        grid=(depth, 3),
        in_specs=[
            pl.BlockSpec((b, D_MODEL), lambda l, j: (0, 0)),
            pl.BlockSpec((1, D_MODEL, D_MODEL), lambda l, j: (l, 0, j)),
            pl.BlockSpec((1, 1, 1, D_MODEL), lambda l, j: (l, j, 0, 0)),
        ],
        out_specs=pl.BlockSpec((1, 1, b, D_MODEL), lambda l, j: (l, j, 0, 0)),
        out_shape=jax.ShapeDtypeStruct((depth, 3, b, D_MODEL), jnp.float32),
        compiler_params=_cparams(2),
        name="adaln_mod",
    )(c, ada_w, ada_b.reshape(depth, 3, 1, D_MODEL))
    return out.reshape(depth, 3, b, 1, D_MODEL)


def _rope_kernel(pos_ref, c_ref, s1_ref, s2_ref):
    pos = pos_ref[0].astype(jnp.float32)
    lane = lax.broadcasted_iota(jnp.int32, (1, LANES), 1)
    cc = lane % HEAD_DIM
    fidx = (cc % (ROT_DIM // 2)).astype(jnp.float32)
    theta = jnp.full((1, LANES), ROPE_THETA, jnp.float32)
    inv_freq = jnp.exp(-(fidx * (2.0 / ROT_DIM)) * jnp.log(theta))
    ang = pos * inv_freq
    cos = jnp.cos(ang)
    sin = jnp.sin(ang)
    c_ref[0] = jnp.where(cc < ROT_DIM, cos, 1.0)
    s1_ref[0] = jnp.where(cc < ROT_DIM // 2, -sin, 0.0)
    s2_ref[0] = jnp.where((cc >= ROT_DIM // 2) & (cc < ROT_DIM), sin, 0.0)


def _rope_tables(positions):
    b, s = positions.shape
    spec = pl.BlockSpec((1, s, LANES), lambda i: (i, 0, 0))
    shp = jax.ShapeDtypeStruct((b, s, LANES), jnp.float32)
    return pl.pallas_call(
        _rope_kernel,
        grid=(b,),
        in_specs=[pl.BlockSpec((1, s, 1), lambda i: (i, 0, 0))],
        out_specs=[spec, spec, spec],
        out_shape=[shp, shp, shp],
        compiler_params=_cparams(1),
        name="rope_tables",
    )(positions.reshape(b, s, 1))


def _inproj_kernel(*refs, rope_slabs, with_forget):
    if with_forget:
        (x_ref, shift_ref, scale_ref, npre_ref, w_ref, wf_ref, c_ref, s1_ref, s2_ref,
         out_ref, fa_ref) = refs
    else:
        (x_ref, shift_ref, scale_ref, npre_ref, w_ref, c_ref, s1_ref, s2_ref, out_ref) = refs
    x = x_ref[0]
    var = jnp.mean(x * x, axis=-1, keepdims=True)
    h = (x * lax.rsqrt(var + NORM_EPS) * npre_ref[...]) * (1.0 + scale_ref[...]) + shift_ref[...]
    hb = h.astype(jnp.bfloat16)
    cos = c_ref[0]
    s1 = s1_ref[0]
    s2 = s2_ref[0]
    for c in range(N_SLABS * LANES // MXU_COLS):
        r = jnp.dot(hb, w_ref[:, c * MXU_COLS:(c + 1) * MXU_COLS],
                    preferred_element_type=jnp.float32)
        for half in range(MXU_COLS // LANES):
            j = c * (MXU_COLS // LANES) + half
            y = r[:, half * LANES:(half + 1) * LANES]
            if j in rope_slabs:
                y = (y * cos + pltpu.roll(y, LANES - ROT_DIM // 2, 1) * s1
                     + pltpu.roll(y, ROT_DIM // 2, 1) * s2)
            out_ref[0, j] = y.astype(jnp.bfloat16)
    if with_forget:
        fa_ref[0] = jnp.dot(hb, wf_ref[...], preferred_element_type=jnp.float32)


def _in_projection(x, shift, scale, npre, w, wf, rope, rope_slabs):
    b, s, d = x.shape
    with_forget = wf is not None
    row = lambda bi, i: (bi, i, 0)
    const2 = lambda bi, i: (0, 0)
    in_specs = [
        pl.BlockSpec((1, TM, d), row),
        pl.BlockSpec((None, 1, d), lambda bi, i: (bi, 0, 0)),
        pl.BlockSpec((None, 1, d), lambda bi, i: (bi, 0, 0)),
        pl.BlockSpec((1, d), const2),
        pl.BlockSpec(w.shape, const2),
    ]
    args = [x, shift, scale, npre.reshape(1, d), w]
    if with_forget:
        in_specs.append(pl.BlockSpec(wf.shape, const2))
        args.append(wf)
    in_specs += [pl.BlockSpec((1, TM, LANES), row)] * 3
    args += list(rope)
    out_specs = [pl.BlockSpec((1, N_SLABS, TM, LANES), lambda bi, i: (bi, 0, i, 0))]
    out_shape = [jax.ShapeDtypeStruct((b, N_SLABS, s, LANES), jnp.bfloat16)]
    if with_forget:
        out_specs.append(pl.BlockSpec((1, TM, LANES), row))
        out_shape.append(jax.ShapeDtypeStruct((b, s, LANES), jnp.float32))
    res = pl.pallas_call(
        functools.partial(_inproj_kernel, rope_slabs=frozenset(rope_slabs),
                          with_forget=with_forget),
        grid=(b, s // TM),
        in_specs=in_specs,
        out_specs=out_specs,
        out_shape=out_shape,
        compiler_params=_cparams(2),
        name="in_projection",
    )(*args)
    return res if with_forget else (res[0], None)


F_BLOCK = 256


def _fgate_kernel(fa_ref, b_ref, g_ref):
    s = fa_ref.shape[1]
    z = fa_ref[0] + b_ref[...]
    logf = jnp.minimum(z, 0.0) - jnp.log(1.0 + jnp.exp(-jnp.abs(z)))
    r_i = lax.broadcasted_iota(jnp.int32, (F_BLOCK, F_BLOCK), 0)
    c_i = lax.broadcasted_iota(jnp.int32, (F_BLOCK, F_BLOCK), 1)
    tri = jnp.where(c_i <= r_i, 1.0, 0.0).astype(jnp.bfloat16)
    lane = lax.broadcasted_iota(jnp.int32, (1, LANES), 1)

    def split3(v):
        hi = v.astype(jnp.bfloat16)
        r1 = v - hi.astype(jnp.float32)
        mid = r1.astype(jnp.bfloat16)
        lo = (r1 - mid.astype(jnp.float32)).astype(jnp.bfloat16)
        return hi, mid, lo

    carry = jnp.zeros((1, LANES), jnp.float32)
    for i in range(s // F_BLOCK):
        blk = logf[i * F_BLOCK:(i + 1) * F_BLOCK]
        parts = jnp.concatenate(split3(blk), axis=1)
        cs3 = jnp.dot(tri, parts, preferred_element_type=jnp.float32)
        f = (cs3[:, :LANES] + cs3[:, LANES:2 * LANES]) + cs3[:, 2 * LANES:] + carry
        carry = f[F_BLOCK - 1:F_BLOCK, :]
        hi, mid, lo = split3(f)
        packed = (jnp.where(lane < N_HEADS_FOX, hi.astype(jnp.float32), 0.0)
                  + pltpu.roll(jnp.where(lane < N_HEADS_FOX, mid.astype(jnp.float32), 0.0),
                               N_HEADS_FOX, 1)
                  + pltpu.roll(jnp.where(lane < N_HEADS_FOX, lo.astype(jnp.float32), 0.0),
                               2 * N_HEADS_FOX, 1))
        g_ref[0, i * F_BLOCK:(i + 1) * F_BLOCK, :] = packed.astype(jnp.bfloat16)


def _forget_prefix(fa, b_forget):
    b, s, _ = fa.shape
    bpad = jnp.zeros((1, LANES), jnp.float32).at[0, :N_HEADS_FOX].set(b_forget)
    return pl.pallas_call(
        _fgate_kernel,
        grid=(b,),
        in_specs=[pl.BlockSpec((1, s, LANES), lambda i: (i, 0, 0)),
                  pl.BlockSpec((1, LANES), lambda i: (0, 0))],
        out_specs=pl.BlockSpec((1, s, LANES), lambda i: (i, 0, 0)),
        out_shape=jax.ShapeDtypeStruct((b, s, LANES), jnp.bfloat16),
        compiler_params=_cparams(1),
        name="forget_prefix",
    )(fa, bpad)


def _n_bias_tables(mode, s_len):
    if mode == "even":
        return 2
    bounded = max((w // d) * d for (w, d) in DILATED_CONFIGS if (w // d) * d < s_len - 1)
    off = 0
    while off * TQ - (TK - 1) <= bounded:
        off += 1
    return off + 1


def _bias_tables(mode, s_len):
    assert TQ == TK
    r = np.arange(TQ)[:, None]
    c = np.arange(TK)[None, :]
    tabs = []
    n_tab = _n_bias_tables(mode, s_len)
    for off in range(n_tab):
        delta = off * TQ + r - c
        if mode == "even":
            mult = (delta >= 0).astype(np.float64)
        else:
            mult = np.zeros((TQ, TK))
            for (w, d) in DILATED_CONFIGS:
                ok = (delta >= 0) & (delta % d == 0) & (delta <= (w // d) * d)
                if off == n_tab - 1:
                    ok = ok & ((w // d) * d >= s_len - 1)
                mult += ok
        with np.errstate(divide="ignore"):
            tabs.append(np.where(mult > 0, np.log(np.maximum(mult, 1e-30)), NEG))
    return jnp.asarray(np.stack(tabs), jnp.float32)


def _attn_kernel(*refs, mode, lam_init):
    if mode == "even":
        (q_ref, k_ref, v_ref, g_ref, aug_ref, tab_ref, par_ref, o_ref,
         kcat, m1, l1, a1, m2, l2, a2) = refs
    else:
        (q_ref, k_ref, v_ref, g_ref, tab_ref, o_ref, m1, l1, a1, m2, l2, a2) = refs
    s_len = q_ref.shape[2]
    u = pl.program_id(1)
    lane = lax.broadcasted_iota(jnp.int32, (1, LANES), 1)
    lo = lane < HEAD_DIM
    nh = N_HEADS_FOX

    if mode == "even":
        is_fox = u < N_UNITS // 2
        fox_f = jnp.where(is_fox, 1.0, 0.0)
        g_all = aug_ref[0].astype(jnp.float32)
        kaug = jnp.where(lane < 3 * nh, -g_all, jnp.where(lane < 6 * nh, 1.0, 0.0)) * fox_f
        kcat[:, :LANES] = k_ref[0, 0]
        kcat[:, LANES:] = kaug.astype(jnp.bfloat16)

    def q_block(qi, carry):
        q0 = pl.multiple_of(qi * TQ, TQ)
        qs = q_ref[0, 0, pl.ds(q0, TQ), :].astype(jnp.float32) * (HEAD_DIM ** -0.5)
        lhs = [jnp.where(lo, qs, 0.0).astype(jnp.bfloat16),
               jnp.where(lo, 0.0, qs).astype(jnp.bfloat16)]
        if mode == "even":
            gq = aug_ref[0, pl.ds(q0, TQ), :].astype(jnp.float32)
            gr = pltpu.roll(gq, 3 * nh, 1)
            for i in range(2):
                h = 2 * u + i
                sel_k = (lane == h) | (lane == h + nh) | (lane == h + 2 * nh)
                sel_q = (lane == h + 3 * nh) | (lane == h + 4 * nh) | (lane == h + 5 * nh)
                qa = jnp.where(sel_k, 1.0, jnp.where(sel_q, gr, 0.0)) * fox_f
                lhs[i] = jnp.concatenate([lhs[i], qa.astype(jnp.bfloat16)], axis=1)
        state = ((m1, l1, a1), (m2, l2, a2))
        for (m_r, l_r, a_r) in state:
            m_r[...] = jnp.full((TQ, LANES), NEG, jnp.float32)
            l_r[...] = jnp.zeros((TQ, LANES), jnp.float32)
            a_r[...] = jnp.zeros((TQ, LANES), jnp.float32)

        def kv_block(kj, c2):
            k0 = pl.multiple_of(kj * TK, TK)
            if mode == "even":
                kb = kcat[pl.ds(k0, TK), :]
            else:
                kb = k_ref[0, 0, pl.ds(k0, TK), :]
            vb = v_ref[0, 0, pl.ds(k0, TK), :]
            bias = tab_ref[jnp.minimum(qi - kj, tab_ref.shape[0] - 1)]
            for i in range(2):
                m_r, l_r, a_r = state[i]
                s = lax.dot_general(lhs[i], kb, (((1,), (1,)), ((), ())),
                                    preferred_element_type=jnp.float32) + bias
                m_prev = m_r[...]
                m_next = jnp.maximum(m_prev, jnp.max(s, axis=1, keepdims=True))
                p = jnp.exp(s - jnp.concatenate([m_next] * (TK // LANES), axis=1))
                alpha = jnp.exp(m_prev - m_next)
                l_r[...] = alpha * l_r[...] + jnp.sum(p, axis=1, keepdims=True)
                a_r[...] = alpha * a_r[...] + jnp.dot(p.astype(jnp.bfloat16), vb,
                                                      preferred_element_type=jnp.float32)
                m_r[...] = m_next
            return c2

        lax.fori_loop(0, qi + 1, kv_block, 0)

        o1 = a1[...] / l1[...]
        o2 = a2[...] / l2[...]
        g = g_ref[0, 0, pl.ds(q0, TQ), :].astype(jnp.float32)
        gate = g * jax.nn.sigmoid(g)
        out = jnp.where(lo, o1, o2)
        if mode == "even":
            par = par_ref[...]
            lam = (jnp.exp(jnp.sum(par[0:1] * par[1:2], axis=1, keepdims=True))
                   - jnp.exp(jnp.sum(par[2:3] * par[3:4], axis=1, keepdims=True)) + lam_init)
            d = o1 - lam * o2
            var = jnp.mean(d * d, axis=-1, keepdims=True)
            dn = (d * lax.rsqrt(var + NORM_EPS) * par[4:5]) * (1.0 - lam_init)
            out = jnp.where(is_fox, out, dn)
        o_ref[0, 0, pl.ds(q0, TQ), :] = (out * gate).astype(jnp.bfloat16)
        return carry

    lax.fori_loop(0, s_len // TQ, q_block, 0)


def _attention(proj, mode, aug=None, params=None, lam_init=0.0):
    b, _, s, _ = proj.shape
    tabs = _bias_tables(mode, s)

    def slab(kind):
        return pl.BlockSpec((1, 1, s, LANES), lambda bi, u: (bi, kind * N_UNITS + u, 0, 0))

    in_specs = [slab(0), slab(1), slab(2), slab(3)]
    args = [proj, proj, proj, proj]
    scratch = []
    if mode == "even":
        in_specs.append(pl.BlockSpec((1, s, LANES), lambda bi, u: (bi, 0, 0)))
        args.append(aug)
        scratch.append(pltpu.VMEM((s, 2 * LANES), jnp.bfloat16))
    in_specs.append(pl.BlockSpec(tabs.shape, lambda bi, u: (0, 0, 0)))
    args.append(tabs)
    if mode == "even":
        in_specs.append(pl.BlockSpec(params.shape, lambda bi, u: (0, 0)))
        args.append(params)
    scratch += [pltpu.VMEM((TQ, LANES), jnp.float32)] * 6
    return pl.pallas_call(
        functools.partial(_attn_kernel, mode=mode, lam_init=lam_init),
        grid=(b, N_UNITS),
        in_specs=in_specs,
        out_specs=pl.BlockSpec((1, 1, s, LANES), lambda bi, u: (bi, u, 0, 0)),
        out_shape=jax.ShapeDtypeStruct((b, N_UNITS, s, LANES), jnp.bfloat16),
        scratch_shapes=scratch,
        compiler_params=_cparams(2),
        name="attention_" + mode,
    )(*args)


def _outproj_kernel(o_ref, w_ref, x_ref, gate_ref, npost_ref, out_ref):
    cat = jnp.concatenate([o_ref[0, j] for j in range(N_UNITS)], axis=1)
    y = jnp.dot(cat, w_ref[...], preferred_element_type=jnp.float32)
    var = jnp.mean(y * y, axis=-1, keepdims=True)
    out_ref[0] = x_ref[0] + gate_ref[...] * (y * lax.rsqrt(var + NORM_EPS) * npost_ref[...])


def _out_projection(o, w, x, gate, npost):
    b, s, d = x.shape
    return pl.pallas_call(
        _outproj_kernel,
        grid=(b, s // TM),
        in_specs=[
            pl.BlockSpec((1, N_UNITS, TM, LANES), lambda bi, i: (bi, 0, i, 0)),
            pl.BlockSpec(w.shape, lambda bi, i: (0, 0)),
            pl.BlockSpec((1, TM, d), lambda bi, i: (bi, i, 0)),
            pl.BlockSpec((None, 1, d), lambda bi, i: (bi, 0, 0)),
            pl.BlockSpec((1, d), lambda bi, i: (0, 0)),
        ],
        out_specs=pl.BlockSpec((1, TM, d), lambda bi, i: (bi, i, 0)),
        out_shape=jax.ShapeDtypeStruct((b, s, d), jnp.float32),
        compiler_params=_cparams(2),
        name="out_projection",
    )(o, w, x, gate, npost.reshape(1, d))


def _even_weight_layout(w_in):
    wf_, wd = N_HEADS_FOX * HEAD_DIM, 4 * 2 * HEAD_DIM
    cuts = np.cumsum([wf_, wf_, wf_, N_HEADS_FOX, wf_, wd, wd, wd, wd])
    qa, ka, va, fa, ga, qd, kd, vd, gd = jnp.split(w_in, [int(v) for v in cuts[:-1]], axis=1)
    main = jnp.concatenate([qa, qd, ka, kd, va, vd, ga, gd], axis=1).astype(jnp.bfloat16)
    wf = jnp.pad(fa, ((0, 0), (0, LANES - N_HEADS_FOX))).astype(jnp.bfloat16)
    return main, wf


def kernel(x, c, positions, norm_pre, norm_post, ada_w, ada_b, ev_w_in, ev_b_forget,
           ev_lambda_q1, ev_lambda_k1, ev_lambda_q2, ev_lambda_k2, ev_subln, ev_w_out,
           od_w_in, od_w_out):
    depth = ada_w.shape[0]
    mod = _adaln_mod(c, ada_w, ada_b)
    rope = _rope_tables(positions)
    half = N_UNITS // 2
    even_rope = set(range(half, N_UNITS)) | set(range(N_UNITS + half, 2 * N_UNITS))
    odd_rope = set(range(2 * N_UNITS))
    for layer in range(depth):
        shift, scale, gate = mod[layer, 0], mod[layer, 1], mod[layer, 2]
        if layer % 2 == 0:
            i = layer // 2
            w_main, wf = _even_weight_layout(ev_w_in[i])
            proj, fa = _in_projection(x, shift, scale, norm_pre[layer], w_main, wf, rope, even_rope)
            aug = _forget_prefix(fa, ev_b_forget[i])
            params = jnp.zeros((8, LANES), jnp.float32)
            params = params.at[0, :HEAD_DIM].set(ev_lambda_q1[i]).at[1, :HEAD_DIM].set(ev_lambda_k1[i])
            params = params.at[2, :HEAD_DIM].set(ev_lambda_q2[i]).at[3, :HEAD_DIM].set(ev_lambda_k2[i])
            params = params.at[4, :].set(ev_subln[i])
            lam_init = 0.8 - 0.6 * math.exp(-0.3 * layer)
            o = _attention(proj, "even", aug=aug, params=params, lam_init=lam_init)
            w_out = ev_w_out[i].astype(jnp.bfloat16)
        else:
            j = layer // 2
            proj, _ = _in_projection(x, shift, scale, norm_pre[layer],
                                     od_w_in[j].astype(jnp.bfloat16), None, rope, odd_rope)
            o = _attention(proj, "odd")
            w_out = od_w_out[j].astype(jnp.bfloat16)
        x = _out_projection(o, w_out, x, gate, norm_post[layer])
    return x
```

```python
import functools
import math

import numpy as np
import jax
import jax.numpy as jnp
from jax import lax
from jax.experimental import pallas as pl
from jax.experimental.pallas import tpu as pltpu

D_MODEL = 1024
HEAD_DIM = 64
N_HEADS_FOX = 8
ROT_DIM = HEAD_DIM // 4
ROPE_THETA = 500000.0
NORM_EPS = 1e-6
DILATED_CONFIGS = ((128, 1), (512, 4), (2048, 16))

LANES = 128
MXU_COLS = 256
N_SLABS = 32
N_UNITS = 8
TQ = 512
TK = 512
TM = 512
NEG = -1e30
LOG2E = math.log2(math.e)
VMEM_LIMIT = 48 * 1024 * 1024


def _cparams(n_axes):
    return pltpu.CompilerParams(
        dimension_semantics=("arbitrary",) * n_axes, vmem_limit_bytes=VMEM_LIMIT)


def _mod_kernel(c_ref, w_ref, b_ref, o_ref):
    c = c_ref[...]
    cond = c * jax.nn.sigmoid(c)
    o_ref[0, 0] = jnp.dot(cond, w_ref[0], preferred_element_type=jnp.float32) + b_ref[0, 0]


def _adaln_mod(c, ada_w, ada_b):
    depth = ada_w.shape[0]
    b = c.shape[0]
    out = pl.pallas_call(
        _mod_kernel,
        grid=(depth, 3),
        in_specs=[
            pl.BlockSpec((b, D_MODEL), lambda l, j: (0, 0)),
            pl.BlockSpec((1, D_MODEL, D_MODEL), lambda l, j: (l, 0, j)),
            pl.BlockSpec((1, 1, 1, D_MODEL), lambda l, j: (l, j, 0, 0)),
        ],
        out_specs=pl.BlockSpec((1, 1, b, D_MODEL), lambda l, j: (l, j, 0, 0)),
        out_shape=jax.ShapeDtypeStruct((depth, 3, b, D_MODEL), jnp.float32),
        compiler_params=_cparams(2),
        name="adaln_mod",
    )(c, ada_w, ada_b.reshape(depth, 3, 1, D_MODEL))
    return out.reshape(depth, 3, b, 1, D_MODEL)


def _rope_kernel(pos_ref, c_ref, s1_ref, s2_ref):
    pos = pos_ref[0].astype(jnp.float32)
    lane = lax.broadcasted_iota(jnp.int32, (1, LANES), 1)
    cc = lane % HEAD_DIM
    fidx = (cc % (ROT_DIM // 2)).astype(jnp.float32)
    theta = jnp.full((1, LANES), ROPE_THETA, jnp.float32)
    inv_freq = jnp.exp(-(fidx * (2.0 / ROT_DIM)) * jnp.log(theta))
    ang = pos * inv_freq
    cos = jnp.cos(ang)
    sin = jnp.sin(ang)
    c_ref[0] = jnp.where(cc < ROT_DIM, cos, 1.0)
    s1_ref[0] = jnp.where(cc < ROT_DIM // 2, -sin, 0.0)
    s2_ref[0] = jnp.where((cc >= ROT_DIM // 2) & (cc < ROT_DIM), sin, 0.0)


def _rope_tables(positions):
    b, s = positions.shape
    spec = pl.BlockSpec((1, s, LANES), lambda i: (i, 0, 0))
    shp = jax.ShapeDtypeStruct((b, s, LANES), jnp.float32)
    return pl.pallas_call(
        _rope_kernel,
        grid=(b,),
        in_specs=[pl.BlockSpec((1, s, 1), lambda i: (i, 0, 0))],
        out_specs=[spec, spec, spec],
        out_shape=[shp, shp, shp],
        compiler_params=_cparams(1),
        name="rope_tables",
    )(positions.reshape(b, s, 1))


def _inproj_kernel(*refs, rope_slabs, with_forget):
    if with_forget:
        (x_ref, shift_ref, scale_ref, npre_ref, w_ref, wf_ref, c_ref, s1_ref, s2_ref,
         out_ref, fa_ref) = refs
    else:
        (x_ref, shift_ref, scale_ref, npre_ref, w_ref, c_ref, s1_ref, s2_ref, out_ref) = refs
    x = x_ref[0]
    var = jnp.mean(x * x, axis=-1, keepdims=True)
    h = (x * lax.rsqrt(var + NORM_EPS) * npre_ref[...]) * (1.0 + scale_ref[...]) + shift_ref[...]
    hb = h.astype(jnp.bfloat16)
    cos = c_ref[0]
    s1 = s1_ref[0]
    s2 = s2_ref[0]
    for c in range(N_SLABS * LANES // MXU_COLS):
        r = jnp.dot(hb, w_ref[:, c * MXU_COLS:(c + 1) * MXU_COLS],
                    preferred_element_type=jnp.float32)
        for half in range(MXU_COLS // LANES):
            j = c * (MXU_COLS // LANES) + half
            y = r[:, half * LANES:(half + 1) * LANES]
            if j in rope_slabs:
                y = (y * cos + pltpu.roll(y, LANES - ROT_DIM // 2, 1) * s1
                     + pltpu.roll(y, ROT_DIM // 2, 1) * s2)
            out_ref[0, j] = y.astype(jnp.bfloat16)
    if with_forget:
        fa_ref[0] = jnp.dot(hb, wf_ref[...], preferred_element_type=jnp.float32)


def _in_projection(x, shift, scale, npre, w, wf, rope, rope_slabs):
    b, s, d = x.shape
    with_forget = wf is not None
    row = lambda bi, i: (bi, i, 0)
    const2 = lambda bi, i: (0, 0)
    in_specs = [
        pl.BlockSpec((1, TM, d), row),
        pl.BlockSpec((None, 1, d), lambda bi, i: (bi, 0, 0)),
        pl.BlockSpec((None, 1, d), lambda bi, i: (bi, 0, 0)),
        pl.BlockSpec((1, d), const2),
        pl.BlockSpec(w.shape, const2),
    ]
    args = [x, shift, scale, npre.reshape(1, d), w]
    if with_forget:
        in_specs.append(pl.BlockSpec(wf.shape, const2))
        args.append(wf)
    in_specs += [pl.BlockSpec((1, TM, LANES), row)] * 3
    args += list(rope)
    out_specs = [pl.BlockSpec((1, N_SLABS, TM, LANES), lambda bi, i: (bi, 0, i, 0))]
    out_shape = [jax.ShapeDtypeStruct((b, N_SLABS, s, LANES), jnp.bfloat16)]
    if with_forget:
        out_specs.append(pl.BlockSpec((1, TM, LANES), row))
        out_shape.append(jax.ShapeDtypeStruct((b, s, LANES), jnp.float32))
    res = pl.pallas_call(
        functools.partial(_inproj_kernel, rope_slabs=frozenset(rope_slabs),
                          with_forget=with_forget),
        grid=(b, s // TM),
        in_specs=in_specs,
        out_specs=out_specs,
        out_shape=out_shape,
        compiler_params=_cparams(2),
        name="in_projection",
    )(*args)
    return res if with_forget else (res[0], None)


F_BLOCK = 256


def _fgate_kernel(fa_ref, b_ref, g_ref):
    s = fa_ref.shape[1]
    z = fa_ref[0] + b_ref[...]
    logf = jnp.minimum(z, 0.0) - jnp.log(1.0 + jnp.exp(-jnp.abs(z)))
    r_i = lax.broadcasted_iota(jnp.int32, (F_BLOCK, F_BLOCK), 0)
    c_i = lax.broadcasted_iota(jnp.int32, (F_BLOCK, F_BLOCK), 1)
    tri = jnp.where(c_i <= r_i, 1.0, 0.0).astype(jnp.bfloat16)
    lane = lax.broadcasted_iota(jnp.int32, (1, LANES), 1)

    def split3(v):
        hi = v.astype(jnp.bfloat16)
        r1 = v - hi.astype(jnp.float32)
        mid = r1.astype(jnp.bfloat16)
        lo = (r1 - mid.astype(jnp.float32)).astype(jnp.bfloat16)
        return hi, mid, lo

    carry = jnp.zeros((1, LANES), jnp.float32)
    for i in range(s // F_BLOCK):
        blk = logf[i * F_BLOCK:(i + 1) * F_BLOCK]
        parts = jnp.concatenate(split3(blk), axis=1)
        cs3 = jnp.dot(tri, parts, preferred_element_type=jnp.float32)
        f = (cs3[:, :LANES] + cs3[:, LANES:2 * LANES]) + cs3[:, 2 * LANES:] + carry
        carry = f[F_BLOCK - 1:F_BLOCK, :]
        hi, mid, lo = split3(f * LOG2E)
        packed = (jnp.where(lane < N_HEADS_FOX, hi.astype(jnp.float32), 0.0)
                  + pltpu.roll(jnp.where(lane < N_HEADS_FOX, mid.astype(jnp.float32), 0.0),
                               N_HEADS_FOX, 1)
                  + pltpu.roll(jnp.where(lane < N_HEADS_FOX, lo.astype(jnp.float32), 0.0),
                               2 * N_HEADS_FOX, 1))
        g_ref[0, i * F_BLOCK:(i + 1) * F_BLOCK, :] = packed.astype(jnp.bfloat16)


def _forget_prefix(fa, b_forget):
    b, s, _ = fa.shape
    bpad = jnp.zeros((1, LANES), jnp.float32).at[0, :N_HEADS_FOX].set(b_forget)
    return pl.pallas_call(
        _fgate_kernel,
        grid=(b,),
        in_specs=[pl.BlockSpec((1, s, LANES), lambda i: (i, 0, 0)),
                  pl.BlockSpec((1, LANES), lambda i: (0, 0))],
        out_specs=pl.BlockSpec((1, s, LANES), lambda i: (i, 0, 0)),
        out_shape=jax.ShapeDtypeStruct((b, s, LANES), jnp.bfloat16),
        compiler_params=_cparams(1),
        name="forget_prefix",
    )(fa, bpad)


V_ROWS = LANES + 16


KPQ = TQ // TK


def _n_bias_tables(mode, s_len):
    if mode == "even":
        return KPQ
    bounded = max((w // d) * d for (w, d) in DILATED_CONFIGS if (w // d) * d < s_len - 1)
    dist = 0
    while dist * TK - (TK - 1) <= bounded:
        dist += 1
    return dist + KPQ


def _bias_tables(mode, s_len):
    kr = np.arange(TK)[:, None]
    qc = np.arange(TQ)[None, :]
    tabs = []
    n_tab = _n_bias_tables(mode, s_len)
    for idx in range(n_tab):
        delta = (idx - (KPQ - 1)) * TK + qc - kr
        off = idx
        if mode == "even":
            mult = (delta >= 0).astype(np.float64)
        else:
            mult = np.zeros((TK, TQ))
            for (w, d) in DILATED_CONFIGS:
                ok = (delta >= 0) & (delta % d == 0) & (delta <= (w // d) * d)
                if off == n_tab - 1:
                    ok = ok & ((w // d) * d >= s_len - 1)
                mult += ok
        tabs.append(np.where(mult > 0, np.log2(np.maximum(mult, 1.0)), NEG))
    return jnp.asarray(np.stack(tabs), jnp.float32)


def _attn_kernel(*refs, mode, lam_init):
    if mode == "even":
        (q_ref, k_ref, v_ref, g_ref, aug_ref, tab_ref, par_ref, o_ref,
         kcat, vt, s_buf, m1, a1, m2, a2) = refs
    else:
        (q_ref, k_ref, v_ref, g_ref, tab_ref, o_ref, vt, s_buf, m1, a1, m2, a2) = refs
    s_len = q_ref.shape[2]
    u = pl.program_id(1)
    lane = lax.broadcasted_iota(jnp.int32, (1, LANES), 1)
    lo = lane < HEAD_DIM
    nh = N_HEADS_FOX
    nt = (((1,), (1,)), ((), ()))

    vtt = v_ref[0, 0].astype(jnp.float32).T
    ones_rows = jnp.where(lax.broadcasted_iota(jnp.int32, (V_ROWS - LANES, TK), 0) == 0, 1.0, 0.0)
    for j in range(s_len // TK):
        vt[j, 0:LANES, :] = vtt[:, j * TK:(j + 1) * TK].astype(jnp.bfloat16)
        vt[j, LANES:V_ROWS, :] = ones_rows.astype(jnp.bfloat16)

    if mode == "even":
        is_fox = u < N_UNITS // 2
        fox_f = jnp.where(is_fox, 1.0, 0.0)
        g_all = aug_ref[0].astype(jnp.float32)
        kaug = jnp.where(lane < 3 * nh, -g_all, jnp.where(lane < 6 * nh, 1.0, 0.0)) * fox_f
        kcat[:, :LANES] = k_ref[0, 0]
        kcat[:, LANES:] = kaug.astype(jnp.bfloat16)

    state = ((m1, a1), (m2, a2))
    n_tab = tab_ref.shape[0]

    def make_lhs(qi):
        q0 = qi * TQ
        qs = q_ref[0, 0, q0:q0 + TQ, :].astype(jnp.float32) * (HEAD_DIM ** -0.5 * LOG2E)
        lhs = [jnp.where(lo, qs, 0.0).astype(jnp.bfloat16),
               jnp.where(lo, 0.0, qs).astype(jnp.bfloat16)]
        if mode == "even":
            gq = aug_ref[0, q0:q0 + TQ, :].astype(jnp.float32)
            gr = pltpu.roll(gq, 3 * nh, 1)
            for i in range(2):
                h = 2 * u + i
                sel_k = (lane == h) | (lane == h + nh) | (lane == h + 2 * nh)
                sel_q = (lane == h + 3 * nh) | (lane == h + 4 * nh) | (lane == h + 5 * nh)
                qa = jnp.where(sel_k, 1.0, jnp.where(sel_q, gr, 0.0)) * fox_f
                lhs[i] = jnp.concatenate([lhs[i], qa.astype(jnp.bfloat16)], axis=1)
        return lhs

    def scores_into(slot, lhs, kt):
        k0 = kt * TK
        if mode == "even":
            kb = kcat[k0:k0 + TK, :]
        else:
            kb = k_ref[0, 0, k0:k0 + TK, :]
        for i in range(2):
            s_buf[slot, i] = lax.dot_general(kb, lhs[i], nt,
                                             preferred_element_type=jnp.float32)

    def consume(slot, kt, bias):
        vtb = vt[kt]
        for i in range(2):
            m_r, a_r = state[i]
            st = s_buf[slot, i]
            if bias is not None:
                st = st + bias
            m_prev = m_r[...]
            m_next = jnp.maximum(m_prev, jnp.max(st, axis=0, keepdims=True))
            pt = jnp.exp2(st - m_next)
            alpha = jnp.exp2(m_prev - m_next)
            a_r[...] = alpha * a_r[...] + jnp.dot(vtb, pt.astype(jnp.bfloat16),
                                                  preferred_element_type=jnp.float32)
            m_r[...] = m_next

    def finish(qi):
        q0 = qi * TQ
        outs = []
        for (m_r, a_r) in state:
            acc = a_r[...]
            inv_l = 1.0 / acc[LANES:LANES + 1, :]
            outs.append((acc[0:LANES, :] * inv_l).T)
        o1, o2 = outs
        g = g_ref[0, 0, q0:q0 + TQ, :].astype(jnp.float32)
        gate = g * jax.nn.sigmoid(g)
        out = jnp.where(lo, o1, o2)
        if mode == "even":
            par = par_ref[...]
            lam = (jnp.exp(jnp.sum(par[0:1] * par[1:2], axis=1, keepdims=True))
                   - jnp.exp(jnp.sum(par[2:3] * par[3:4], axis=1, keepdims=True)) + lam_init)
            d = o1 - lam * o2
            var = jnp.mean(d * d, axis=-1, keepdims=True)
            dn = (d * lax.rsqrt(var + NORM_EPS) * par[4:5]) * (1.0 - lam_init)
            out = jnp.where(is_fox, out, dn)
        o_ref[0, 0, q0:q0 + TQ, :] = (out * gate).astype(jnp.bfloat16)

    pairs = [(qi, kt) for qi in range(s_len // TQ) for kt in range(qi + 1)]
    lhs = make_lhs(0)
    scores_into(0, lhs, 0)
    for n, (qi, kt) in enumerate(pairs):
        if kt == 0:
            for (m_r, a_r) in state:
                m_r[...] = jnp.full((1, TQ), NEG, jnp.float32)
                a_r[...] = jnp.zeros((V_ROWS, TQ), jnp.float32)
        dist = qi - kt
        if mode == "even" and dist > 0:
            bias = None
        else:
            bias = tab_ref[min(dist, n_tab - 1)]
        consume(n % 2, kt, bias)
        if n + 1 < len(pairs):
            nqi, nkt = pairs[n + 1]
            if nkt == 0:
                lhs = make_lhs(nqi)
            scores_into((n + 1) % 2, lhs, nkt)
        if kt == qi:
            finish(qi)


def _attention(proj, mode, aug=None, params=None, lam_init=0.0):
    b, _, s, _ = proj.shape
    tabs = _bias_tables(mode, s)

    def slab(kind):
        return pl.BlockSpec((1, 1, s, LANES), lambda bi, u: (bi, kind * N_UNITS + u, 0, 0))

    in_specs = [slab(0), slab(1), slab(2), slab(3)]
    args = [proj, proj, proj, proj]
    scratch = []
    if mode == "even":
        in_specs.append(pl.BlockSpec((1, s, LANES), lambda bi, u: (bi, 0, 0)))
        args.append(aug)
        scratch.append(pltpu.VMEM((s, 2 * LANES), jnp.bfloat16))
    in_specs.append(pl.BlockSpec(tabs.shape, lambda bi, u: (0, 0, 0)))
    args.append(tabs)
    if mode == "even":
        in_specs.append(pl.BlockSpec(params.shape, lambda bi, u: (0, 0)))
        args.append(params)
    scratch.append(pltpu.VMEM((s // TK, V_ROWS, TK), jnp.bfloat16))
    scratch.append(pltpu.VMEM((2, 2, TK, TQ), jnp.float32))
    scratch += [pltpu.VMEM((1, TQ), jnp.float32), pltpu.VMEM((V_ROWS, TQ), jnp.float32)] * 2
    return pl.pallas_call(
        functools.partial(_attn_kernel, mode=mode, lam_init=lam_init),
        grid=(b, N_UNITS),
        in_specs=in_specs,
        out_specs=pl.BlockSpec((1, 1, s, LANES), lambda bi, u: (bi, u, 0, 0)),
        out_shape=jax.ShapeDtypeStruct((b, N_UNITS, s, LANES), jnp.bfloat16),
        scratch_shapes=scratch,
        compiler_params=_cparams(2),
        name="attention_" + mode,
    )(*args)


def _outproj_kernel(o_ref, w_ref, x_ref, gate_ref, npost_ref, out_ref):
    cat = jnp.concatenate([o_ref[0, j] for j in range(N_UNITS)], axis=1)
    y = jnp.dot(cat, w_ref[...], preferred_element_type=jnp.float32)
    var = jnp.mean(y * y, axis=-1, keepdims=True)
    out_ref[0] = x_ref[0] + gate_ref[...] * (y * lax.rsqrt(var + NORM_EPS) * npost_ref[...])


def _out_projection(o, w, x, gate, npost):
    b, s, d = x.shape
    return pl.pallas_call(
        _outproj_kernel,
        grid=(b, s // TM),
        in_specs=[
            pl.BlockSpec((1, N_UNITS, TM, LANES), lambda bi, i: (bi, 0, i, 0)),
            pl.BlockSpec(w.shape, lambda bi, i: (0, 0)),
            pl.BlockSpec((1, TM, d), lambda bi, i: (bi, i, 0)),
            pl.BlockSpec((None, 1, d), lambda bi, i: (bi, 0, 0)),
            pl.BlockSpec((1, d), lambda bi, i: (0, 0)),
        ],
        out_specs=pl.BlockSpec((1, TM, d), lambda bi, i: (bi, i, 0)),
        out_shape=jax.ShapeDtypeStruct((b, s, d), jnp.float32),
        compiler_params=_cparams(2),
        name="out_projection",
    )(o, w, x, gate, npost.reshape(1, d))


def _even_weight_layout(w_in):
    wf_, wd = N_HEADS_FOX * HEAD_DIM, 4 * 2 * HEAD_DIM
    cuts = np.cumsum([wf_, wf_, wf_, N_HEADS_FOX, wf_, wd, wd, wd, wd])
    qa, ka, va, fa, ga, qd, kd, vd, gd = jnp.split(w_in, [int(v) for v in cuts[:-1]], axis=1)
    main = jnp.concatenate([qa, qd, ka, kd, va, vd, ga, gd], axis=1).astype(jnp.bfloat16)
    wf = jnp.pad(fa, ((0, 0), (0, LANES - N_HEADS_FOX))).astype(jnp.bfloat16)
    return main, wf


def kernel(x, c, positions, norm_pre, norm_post, ada_w, ada_b, ev_w_in, ev_b_forget,
           ev_lambda_q1, ev_lambda_k1, ev_lambda_q2, ev_lambda_k2, ev_subln, ev_w_out,
           od_w_in, od_w_out):
    depth = ada_w.shape[0]
    mod = _adaln_mod(c, ada_w, ada_b)
    rope = _rope_tables(positions)
    half = N_UNITS // 2
    even_rope = set(range(half, N_UNITS)) | set(range(N_UNITS + half, 2 * N_UNITS))
    odd_rope = set(range(2 * N_UNITS))
    for layer in range(depth):
        shift, scale, gate = mod[layer, 0], mod[layer, 1], mod[layer, 2]
        if layer % 2 == 0:
            i = layer // 2
            w_main, wf = _even_weight_layout(ev_w_in[i])
            proj, fa = _in_projection(x, shift, scale, norm_pre[layer], w_main, wf, rope, even_rope)
            aug = _forget_prefix(fa, ev_b_forget[i])
            params = jnp.zeros((8, LANES), jnp.float32)
            params = params.at[0, :HEAD_DIM].set(ev_lambda_q1[i]).at[1, :HEAD_DIM].set(ev_lambda_k1[i])
            params = params.at[2, :HEAD_DIM].set(ev_lambda_q2[i]).at[3, :HEAD_DIM].set(ev_lambda_k2[i])
            params = params.at[4, :].set(ev_subln[i])
            lam_init = 0.8 - 0.6 * math.exp(-0.3 * layer)
            o = _attention(proj, "even", aug=aug, params=params, lam_init=lam_init)
            w_out = ev_w_out[i].astype(jnp.bfloat16)
        else:
            j = layer // 2
            proj, _ = _in_projection(x, shift, scale, norm_pre[layer],
                                     od_w_in[j].astype(jnp.bfloat16), None, rope, odd_rope)
            o = _attention(proj, "odd")
            w_out = od_w_out[j].astype(jnp.bfloat16)
        x = _out_projection(o, w_out, x, gate, norm_post[layer])
    return x
```

```python
import functools
import math

import numpy as np
import jax
import jax.numpy as jnp
from jax import lax
from jax.experimental import pallas as pl
from jax.experimental.pallas import tpu as pltpu

D_MODEL = 1024
HEAD_DIM = 64
N_HEADS_FOX = 8
ROT_DIM = HEAD_DIM // 4
ROPE_THETA = 500000.0
NORM_EPS = 1e-6
DILATED_CONFIGS = ((128, 1), (512, 4), (2048, 16))

LANES = 128
MXU_COLS = 256
N_SLABS = 32
N_UNITS = 8
TQ = 512
TK = 512
TM = 512
NEG = -1e30
LOG2E = math.log2(math.e)
VMEM_LIMIT = 48 * 1024 * 1024


def _cparams(n_axes):
    return pltpu.CompilerParams(
        dimension_semantics=("arbitrary",) * n_axes, vmem_limit_bytes=VMEM_LIMIT)


def _mod_kernel(c_ref, w_ref, b_ref, o_ref):
    c = c_ref[...]
    cond = c * jax.nn.sigmoid(c)
    o_ref[0, 0] = jnp.dot(cond, w_ref[0], preferred_element_type=jnp.float32) + b_ref[0, 0]


def _adaln_mod(c, ada_w, ada_b):
    depth = ada_w.shape[0]
    b = c.shape[0]
    out = pl.pallas_call(
        _mod_kernel,
        grid=(depth, 3),
        in_specs=[
            pl.BlockSpec((b, D_MODEL), lambda l, j: (0, 0)),
            pl.BlockSpec((1, D_MODEL, D_MODEL), lambda l, j: (l, 0, j)),
            pl.BlockSpec((1, 1, 1, D_MODEL), lambda l, j: (l, j, 0, 0)),
        ],
        out_specs=pl.BlockSpec((1, 1, b, D_MODEL), lambda l, j: (l, j, 0, 0)),
        out_shape=jax.ShapeDtypeStruct((depth, 3, b, D_MODEL), jnp.float32),
        compiler_params=_cparams(2),
        name="adaln_mod",
    )(c, ada_w, ada_b.reshape(depth, 3, 1, D_MODEL))
    return out.reshape(depth, 3, b, 1, D_MODEL)


def _rope_kernel(pos_ref, c_ref, s1_ref, s2_ref):
    pos = pos_ref[0].astype(jnp.float32)
    lane = lax.broadcasted_iota(jnp.int32, (1, LANES), 1)
    cc = lane % HEAD_DIM
    fidx = (cc % (ROT_DIM // 2)).astype(jnp.float32)
    theta = jnp.full((1, LANES), ROPE_THETA, jnp.float32)
    inv_freq = jnp.exp(-(fidx * (2.0 / ROT_DIM)) * jnp.log(theta))
    ang = pos * inv_freq
    cos = jnp.cos(ang)
    sin = jnp.sin(ang)
    c_ref[0] = jnp.where(cc < ROT_DIM, cos, 1.0)
    s1_ref[0] = jnp.where(cc < ROT_DIM // 2, -sin, 0.0)
    s2_ref[0] = jnp.where((cc >= ROT_DIM // 2) & (cc < ROT_DIM), sin, 0.0)


def _rope_tables(positions):
    b, s = positions.shape
    spec = pl.BlockSpec((1, s, LANES), lambda i: (i, 0, 0))
    shp = jax.ShapeDtypeStruct((b, s, LANES), jnp.float32)
    return pl.pallas_call(
        _rope_kernel,
        grid=(b,),
        in_specs=[pl.BlockSpec((1, s, 1), lambda i: (i, 0, 0))],
        out_specs=[spec, spec, spec],
        out_shape=[shp, shp, shp],
        compiler_params=_cparams(1),
        name="rope_tables",
    )(positions.reshape(b, s, 1))


def _inproj_kernel(*refs, rope_slabs, with_forget):
    if with_forget:
        (x_ref, shift_ref, scale_ref, npre_ref, w_ref, wf_ref, c_ref, s1_ref, s2_ref,
         out_ref, fa_ref) = refs
    else:
        (x_ref, shift_ref, scale_ref, npre_ref, w_ref, c_ref, s1_ref, s2_ref, out_ref) = refs
    x = x_ref[0]
    var = jnp.mean(x * x, axis=-1, keepdims=True)
    h = (x * lax.rsqrt(var + NORM_EPS) * npre_ref[...]) * (1.0 + scale_ref[...]) + shift_ref[...]
    hb = h.astype(jnp.bfloat16)
    cos = c_ref[0]
    s1 = s1_ref[0]
    s2 = s2_ref[0]
    for c in range(N_SLABS * LANES // MXU_COLS):
        r = jnp.dot(hb, w_ref[:, c * MXU_COLS:(c + 1) * MXU_COLS],
                    preferred_element_type=jnp.float32)
        for half in range(MXU_COLS // LANES):
            j = c * (MXU_COLS // LANES) + half
            y = r[:, half * LANES:(half + 1) * LANES]
            if j in rope_slabs:
                y = (y * cos + pltpu.roll(y, LANES - ROT_DIM // 2, 1) * s1
                     + pltpu.roll(y, ROT_DIM // 2, 1) * s2)
            out_ref[0, j] = y.astype(jnp.bfloat16)
    if with_forget:
        fa_ref[0] = jnp.dot(hb, wf_ref[...], preferred_element_type=jnp.float32)


def _in_projection(x, shift, scale, npre, w, wf, rope, rope_slabs):
    b, s, d = x.shape
    with_forget = wf is not None
    row = lambda bi, i: (bi, i, 0)
    const2 = lambda bi, i: (0, 0)
    in_specs = [
        pl.BlockSpec((1, TM, d), row),
        pl.BlockSpec((None, 1, d), lambda bi, i: (bi, 0, 0)),
        pl.BlockSpec((None, 1, d), lambda bi, i: (bi, 0, 0)),
        pl.BlockSpec((1, d), const2),
        pl.BlockSpec(w.shape, const2),
    ]
    args = [x, shift, scale, npre.reshape(1, d), w]
    if with_forget:
        in_specs.append(pl.BlockSpec(wf.shape, const2))
        args.append(wf)
    in_specs += [pl.BlockSpec((1, TM, LANES), row)] * 3
    args += list(rope)
    out_specs = [pl.BlockSpec((1, N_SLABS, TM, LANES), lambda bi, i: (bi, 0, i, 0))]
    out_shape = [jax.ShapeDtypeStruct((b, N_SLABS, s, LANES), jnp.bfloat16)]
    if with_forget:
        out_specs.append(pl.BlockSpec((1, TM, LANES), row))
        out_shape.append(jax.ShapeDtypeStruct((b, s, LANES), jnp.float32))
    res = pl.pallas_call(
        functools.partial(_inproj_kernel, rope_slabs=frozenset(rope_slabs),
                          with_forget=with_forget),
        grid=(b, s // TM),
        in_specs=in_specs,
        out_specs=out_specs,
        out_shape=out_shape,
        compiler_params=_cparams(2),
        name="in_projection",
    )(*args)
    return res if with_forget else (res[0], None)


F_BLOCK = 256


def _fgate_kernel(fa_ref, b_ref, g_ref):
    s = fa_ref.shape[1]
    z = fa_ref[0] + b_ref[...]
    logf = jnp.minimum(z, 0.0) - jnp.log(1.0 + jnp.exp(-jnp.abs(z)))
    r_i = lax.broadcasted_iota(jnp.int32, (F_BLOCK, F_BLOCK), 0)
    c_i = lax.broadcasted_iota(jnp.int32, (F_BLOCK, F_BLOCK), 1)
    tri = jnp.where(c_i <= r_i, 1.0, 0.0).astype(jnp.bfloat16)
    lane = lax.broadcasted_iota(jnp.int32, (1, LANES), 1)

    def split3(v):
        hi = v.astype(jnp.bfloat16)
        r1 = v - hi.astype(jnp.float32)
        mid = r1.astype(jnp.bfloat16)
        lo = (r1 - mid.astype(jnp.float32)).astype(jnp.bfloat16)
        return hi, mid, lo

    carry = jnp.zeros((1, LANES), jnp.float32)
    for i in range(s // F_BLOCK):
        blk = logf[i * F_BLOCK:(i + 1) * F_BLOCK]
        parts = jnp.concatenate(split3(blk), axis=1)
        cs3 = jnp.dot(tri, parts, preferred_element_type=jnp.float32)
        f = (cs3[:, :LANES] + cs3[:, LANES:2 * LANES]) + cs3[:, 2 * LANES:] + carry
        carry = f[F_BLOCK - 1:F_BLOCK, :]
        hi, mid, lo = split3(f * LOG2E)
        packed = (jnp.where(lane < N_HEADS_FOX, hi.astype(jnp.float32), 0.0)
                  + pltpu.roll(jnp.where(lane < N_HEADS_FOX, mid.astype(jnp.float32), 0.0),
                               N_HEADS_FOX, 1)
                  + pltpu.roll(jnp.where(lane < N_HEADS_FOX, lo.astype(jnp.float32), 0.0),
                               2 * N_HEADS_FOX, 1))
        g_ref[0, i * F_BLOCK:(i + 1) * F_BLOCK, :] = packed.astype(jnp.bfloat16)


def _forget_prefix(fa, b_forget):
    b, s, _ = fa.shape
    bpad = jnp.zeros((1, LANES), jnp.float32).at[0, :N_HEADS_FOX].set(b_forget)
    return pl.pallas_call(
        _fgate_kernel,
        grid=(b,),
        in_specs=[pl.BlockSpec((1, s, LANES), lambda i: (i, 0, 0)),
                  pl.BlockSpec((1, LANES), lambda i: (0, 0))],
        out_specs=pl.BlockSpec((1, s, LANES), lambda i: (i, 0, 0)),
        out_shape=jax.ShapeDtypeStruct((b, s, LANES), jnp.bfloat16),
        compiler_params=_cparams(1),
        name="forget_prefix",
    )(fa, bpad)


V_ROWS = LANES + 16


KPQ = TQ // TK


def _n_bias_tables(mode, s_len):
    if mode == "even":
        return KPQ
    bounded = max((w // d) * d for (w, d) in DILATED_CONFIGS if (w // d) * d < s_len - 1)
    dist = 0
    while dist * TK - (TK - 1) <= bounded:
        dist += 1
    return dist + KPQ


def _bias_tables(mode, s_len):
    kr = np.arange(TK)[:, None]
    qc = np.arange(TQ)[None, :]
    tabs = []
    n_tab = _n_bias_tables(mode, s_len)
    for idx in range(n_tab):
        delta = (idx - (KPQ - 1)) * TK + qc - kr
        off = idx
        if mode == "even":
            mult = (delta >= 0).astype(np.float64)
        else:
            mult = np.zeros((TK, TQ))
            for (w, d) in DILATED_CONFIGS:
                ok = (delta >= 0) & (delta % d == 0) & (delta <= (w // d) * d)
                if off == n_tab - 1:
                    ok = ok & ((w // d) * d >= s_len - 1)
                mult += ok
        tabs.append(np.where(mult > 0, np.log2(np.maximum(mult, 1.0)), NEG))
    return jnp.asarray(np.stack(tabs), jnp.float32)


def _attn_kernel(*refs, mode, lam_init):
    if mode == "even":
        (q_ref, k_ref, v_ref, g_ref, aug_ref, tab_ref, par_ref, o_ref,
         kcat, vt, sb00, sb01, sb10, sb11, cm_buf, m1, a1, m2, a2) = refs
    else:
        (q_ref, k_ref, v_ref, g_ref, tab_ref, o_ref, vt, sb00, sb01, sb10, sb11, cm_buf, m1, a1, m2, a2) = refs
    s_len = q_ref.shape[2]
    u = pl.program_id(1)
    lane = lax.broadcasted_iota(jnp.int32, (1, LANES), 1)
    lo = lane < HEAD_DIM
    nh = N_HEADS_FOX
    nt = (((1,), (1,)), ((), ()))

    vtt = v_ref[0, 0].astype(jnp.float32).T
    ones_rows = jnp.where(lax.broadcasted_iota(jnp.int32, (V_ROWS - LANES, TK), 0) == 0, 1.0, 0.0)
    for j in range(s_len // TK):
        vt[j, 0:LANES, :] = vtt[:, j * TK:(j + 1) * TK].astype(jnp.bfloat16)
        vt[j, LANES:V_ROWS, :] = ones_rows.astype(jnp.bfloat16)

    if mode == "even":
        is_fox = u < N_UNITS // 2
        fox_f = jnp.where(is_fox, 1.0, 0.0)
        g_all = aug_ref[0].astype(jnp.float32)
        kaug = jnp.where(lane < 3 * nh, -g_all, jnp.where(lane < 6 * nh, 1.0, 0.0)) * fox_f
        kcat[:, :LANES] = k_ref[0, 0]
        kcat[:, LANES:] = kaug.astype(jnp.bfloat16)

    state = ((m1, a1), (m2, a2))
    n_tab = tab_ref.shape[0]

    def make_lhs(qi):
        q0 = qi * TQ
        qs = q_ref[0, 0, q0:q0 + TQ, :].astype(jnp.float32) * (HEAD_DIM ** -0.5 * LOG2E)
        lhs = [jnp.where(lo, qs, 0.0).astype(jnp.bfloat16),
               jnp.where(lo, 0.0, qs).astype(jnp.bfloat16)]
        if mode == "even":
            gq = aug_ref[0, q0:q0 + TQ, :].astype(jnp.float32)
            gr = pltpu.roll(gq, 3 * nh, 1)
            for i in range(2):
                h = 2 * u + i
                sel_k = (lane == h) | (lane == h + nh) | (lane == h + 2 * nh)
                sel_q = (lane == h + 3 * nh) | (lane == h + 4 * nh) | (lane == h + 5 * nh)
                qa = jnp.where(sel_k, 1.0, jnp.where(sel_q, gr, 0.0)) * fox_f
                lhs[i] = jnp.concatenate([lhs[i], qa.astype(jnp.bfloat16)], axis=1)
        return lhs

    half = TK // 2
    assert TQ == TK
    s_bufs = ((sb00, sb01), (sb10, sb11))
    z0 = pl.multiple_of(jnp.minimum(pl.program_id(0), 0), TK)

    def scores_into(slot, lhs, qi, kt):
        k0 = kt * TK
        kref = kcat if mode == "even" else k_ref.at[0, 0]
        dist = qi - kt
        for i in range(2):
            if dist > 0:
                st = lax.dot_general(kref[k0:k0 + TK, :], lhs[i], nt,
                                     preferred_element_type=jnp.float32)
                if mode != "even":
                    st = st + tab_ref[min(dist, n_tab - 1)]
                s_bufs[slot][i][pl.ds(z0, TK), :] = st
                cm_buf[slot, i] = jnp.max(st, axis=0, keepdims=True)
            else:
                sa = lax.dot_general(kref[k0:k0 + half, :], lhs[i], nt,
                                     preferred_element_type=jnp.float32) + tab_ref[0, 0:half, :]
                sb = lax.dot_general(kref[k0 + half:k0 + TK, :], lhs[i][half:, :], nt,
                                     preferred_element_type=jnp.float32) + tab_ref[0, half:, half:]
                s_bufs[slot][i][pl.ds(z0, half), :] = sa
                s_bufs[slot][i][pl.ds(z0 + half, half), half:] = sb
                ca = jnp.max(sa, axis=0, keepdims=True)
                cb = jnp.max(sb, axis=0, keepdims=True)
                cm_buf[slot, i] = jnp.concatenate(
                    [ca[:, :half], jnp.maximum(ca[:, half:], cb)], axis=1)

    def consume(slot, qi, kt):
        for i in range(2):
            m_r, a_r = state[i]
            m_prev = m_r[...]
            m_next = jnp.maximum(m_prev, cm_buf[slot, i])
            alpha = jnp.exp2(m_prev - m_next)
            if qi > kt:
                pt = jnp.exp2(s_bufs[slot][i][pl.ds(z0, TK), :] - m_next).astype(jnp.bfloat16)
                a_r[...] = alpha * a_r[...] + jnp.dot(vt[kt], pt, preferred_element_type=jnp.float32)
            else:
                pa = jnp.exp2(s_bufs[slot][i][pl.ds(z0, half), :] - m_next).astype(jnp.bfloat16)
                pb = jnp.exp2(s_bufs[slot][i][pl.ds(z0 + half, half), half:] - m_next[:, half:]).astype(jnp.bfloat16)
                pva = jnp.dot(vt[kt, :, 0:half], pa, preferred_element_type=jnp.float32)
                pvb = jnp.dot(vt[kt, :, half:], pb, preferred_element_type=jnp.float32)
                a_r[:, 0:half] = alpha[:, :half] * a_r[:, 0:half] + pva[:, :half]
                a_r[:, half:] = alpha[:, half:] * a_r[:, half:] + (pva[:, half:] + pvb)
            m_r[...] = m_next

    def finish(qi):
        q0 = qi * TQ
        outs = []
        for (m_r, a_r) in state:
            acc = a_r[...]
            inv_l = 1.0 / acc[LANES:LANES + 1, :]
            outs.append((acc[0:LANES, :] * inv_l).T)
        o1, o2 = outs
        g = g_ref[0, 0, q0:q0 + TQ, :].astype(jnp.float32)
        gate = g * jax.nn.sigmoid(g)
        out = jnp.where(lo, o1, o2)
        if mode == "even":
            par = par_ref[...]
            lam = (jnp.exp(jnp.sum(par[0:1] * par[1:2], axis=1, keepdims=True))
                   - jnp.exp(jnp.sum(par[2:3] * par[3:4], axis=1, keepdims=True)) + lam_init)
            d = o1 - lam * o2
            var = jnp.mean(d * d, axis=-1, keepdims=True)
            dn = (d * lax.rsqrt(var + NORM_EPS) * par[4:5]) * (1.0 - lam_init)
            out = jnp.where(is_fox, out, dn)
        o_ref[0, 0, q0:q0 + TQ, :] = (out * gate).astype(jnp.bfloat16)

    pairs = [(qi, kt) for qi in range(s_len // TQ) for kt in range(qi + 1)]
    lhs = make_lhs(0)
    scores_into(0, lhs, 0, 0)
    for n, (qi, kt) in enumerate(pairs):
        if kt == 0:
            for (m_r, a_r) in state:
                m_r[...] = jnp.full((1, TQ), NEG, jnp.float32)
                a_r[...] = jnp.zeros((V_ROWS, TQ), jnp.float32)
        if n + 1 < len(pairs):
            nqi, nkt = pairs[n + 1]
            if nkt == 0:
                lhs = make_lhs(nqi)
            scores_into((n + 1) % 2, lhs, nqi, nkt)
        consume(n % 2, qi, kt)
        if kt == qi:
            finish(qi)


def _attention(proj, mode, aug=None, params=None, lam_init=0.0):
    b, _, s, _ = proj.shape
    tabs = _bias_tables(mode, s)

    def slab(kind):
        return pl.BlockSpec((1, 1, s, LANES), lambda bi, u: (bi, kind * N_UNITS + u, 0, 0))

    in_specs = [slab(0), slab(1), slab(2), slab(3)]
    args = [proj, proj, proj, proj]
    scratch = []
    if mode == "even":
        in_specs.append(pl.BlockSpec((1, s, LANES), lambda bi, u: (bi, 0, 0)))
        args.append(aug)
        scratch.append(pltpu.VMEM((s, 2 * LANES), jnp.bfloat16))
    in_specs.append(pl.BlockSpec(tabs.shape, lambda bi, u: (0, 0, 0)))
    args.append(tabs)
    if mode == "even":
        in_specs.append(pl.BlockSpec(params.shape, lambda bi, u: (0, 0)))
        args.append(params)
    scratch.append(pltpu.VMEM((s // TK, V_ROWS, TK), jnp.bfloat16))
    scratch += [pltpu.VMEM((TK, TQ), jnp.float32)] * 4
    scratch.append(pltpu.VMEM((2, 2, 1, TQ), jnp.float32))
    scratch += [pltpu.VMEM((1, TQ), jnp.float32), pltpu.VMEM((V_ROWS, TQ), jnp.float32)] * 2
    return pl.pallas_call(
        functools.partial(_attn_kernel, mode=mode, lam_init=lam_init),
        grid=(b, N_UNITS),
        in_specs=in_specs,
        out_specs=pl.BlockSpec((1, 1, s, LANES), lambda bi, u: (bi, u, 0, 0)),
        out_shape=jax.ShapeDtypeStruct((b, N_UNITS, s, LANES), jnp.bfloat16),
        scratch_shapes=scratch,
        compiler_params=_cparams(2),
        name="attention_" + mode,
    )(*args)


def _outproj_kernel(o_ref, w_ref, x_ref, gate_ref, npost_ref, out_ref):
    cat = jnp.concatenate([o_ref[0, j] for j in range(N_UNITS)], axis=1)
    y = jnp.dot(cat, w_ref[...], preferred_element_type=jnp.float32)
    var = jnp.mean(y * y, axis=-1, keepdims=True)
    out_ref[0] = x_ref[0] + gate_ref[...] * (y * lax.rsqrt(var + NORM_EPS) * npost_ref[...])


def _out_projection(o, w, x, gate, npost):
    b, s, d = x.shape
    return pl.pallas_call(
        _outproj_kernel,
        grid=(b, s // TM),
        in_specs=[
            pl.BlockSpec((1, N_UNITS, TM, LANES), lambda bi, i: (bi, 0, i, 0)),
            pl.BlockSpec(w.shape, lambda bi, i: (0, 0)),
            pl.BlockSpec((1, TM, d), lambda bi, i: (bi, i, 0)),
            pl.BlockSpec((None, 1, d), lambda bi, i: (bi, 0, 0)),
            pl.BlockSpec((1, d), lambda bi, i: (0, 0)),
        ],
        out_specs=pl.BlockSpec((1, TM, d), lambda bi, i: (bi, i, 0)),
        out_shape=jax.ShapeDtypeStruct((b, s, d), jnp.float32),
        compiler_params=_cparams(2),
        name="out_projection",
    )(o, w, x, gate, npost.reshape(1, d))


def _even_weight_layout(w_in):
    wf_, wd = N_HEADS_FOX * HEAD_DIM, 4 * 2 * HEAD_DIM
    cuts = np.cumsum([wf_, wf_, wf_, N_HEADS_FOX, wf_, wd, wd, wd, wd])
    qa, ka, va, fa, ga, qd, kd, vd, gd = jnp.split(w_in, [int(v) for v in cuts[:-1]], axis=1)
    main = jnp.concatenate([qa, qd, ka, kd, va, vd, ga, gd], axis=1).astype(jnp.bfloat16)
    wf = jnp.pad(fa, ((0, 0), (0, LANES - N_HEADS_FOX))).astype(jnp.bfloat16)
    return main, wf


def kernel(x, c, positions, norm_pre, norm_post, ada_w, ada_b, ev_w_in, ev_b_forget,
           ev_lambda_q1, ev_lambda_k1, ev_lambda_q2, ev_lambda_k2, ev_subln, ev_w_out,
           od_w_in, od_w_out):
    depth = ada_w.shape[0]
    mod = _adaln_mod(c, ada_w, ada_b)
    rope = _rope_tables(positions)
    half = N_UNITS // 2
    even_rope = set(range(half, N_UNITS)) | set(range(N_UNITS + half, 2 * N_UNITS))
    odd_rope = set(range(2 * N_UNITS))
    for layer in range(depth):
        shift, scale, gate = mod[layer, 0], mod[layer, 1], mod[layer, 2]
        if layer % 2 == 0:
            i = layer // 2
            w_main, wf = _even_weight_layout(ev_w_in[i])
            proj, fa = _in_projection(x, shift, scale, norm_pre[layer], w_main, wf, rope, even_rope)
            aug = _forget_prefix(fa, ev_b_forget[i])
            params = jnp.zeros((8, LANES), jnp.float32)
            params = params.at[0, :HEAD_DIM].set(ev_lambda_q1[i]).at[1, :HEAD_DIM].set(ev_lambda_k1[i])
            params = params.at[2, :HEAD_DIM].set(ev_lambda_q2[i]).at[3, :HEAD_DIM].set(ev_lambda_k2[i])
            params = params.at[4, :].set(ev_subln[i])
            lam_init = 0.8 - 0.6 * math.exp(-0.3 * layer)
            o = _attention(proj, "even", aug=aug, params=params, lam_init=lam_init)
            w_out = ev_w_out[i].astype(jnp.bfloat16)
        else:
            j = layer // 2
            proj, _ = _in_projection(x, shift, scale, norm_pre[layer],
                                     od_w_in[j].astype(jnp.bfloat16), None, rope, odd_rope)
            o = _attention(proj, "odd")
            w_out = od_w_out[j].astype(jnp.bfloat16)
        x = _out_projection(o, w_out, x, gate, norm_post[layer])
    return x
```

```python
import functools
import math

import numpy as np
import jax
import jax.numpy as jnp
from jax import lax
from jax.experimental import pallas as pl
from jax.experimental.pallas import tpu as pltpu

D_MODEL = 1024
HEAD_DIM = 64
N_HEADS_FOX = 8
ROT_DIM = HEAD_DIM // 4
ROPE_THETA = 500000.0
NORM_EPS = 1e-6
DILATED_CONFIGS = ((128, 1), (512, 4), (2048, 16))

LANES = 128
MXU_COLS = 256
N_SLABS = 32
N_UNITS = 8
TQ = 512
TK = 512
TM = 512
NEG = -1e30
LOG2E = math.log2(math.e)
VMEM_LIMIT = 48 * 1024 * 1024


def _cparams(n_axes):
    return pltpu.CompilerParams(
        dimension_semantics=("arbitrary",) * n_axes, vmem_limit_bytes=VMEM_LIMIT)


def _mod_kernel(c_ref, w_ref, b_ref, o_ref):
    c = c_ref[...]
    cond = c * jax.nn.sigmoid(c)
    o_ref[0, 0] = jnp.dot(cond, w_ref[0], preferred_element_type=jnp.float32) + b_ref[0, 0]


def _adaln_mod(c, ada_w, ada_b):
    depth = ada_w.shape[0]
    b = c.shape[0]
    out = pl.pallas_call(
        _mod_kernel,
        grid=(depth, 3),
        in_specs=[
            pl.BlockSpec((b, D_MODEL), lambda l, j: (0, 0)),
            pl.BlockSpec((1, D_MODEL, D_MODEL), lambda l, j: (l, 0, j)),
            pl.BlockSpec((1, 1, 1, D_MODEL), lambda l, j: (l, j, 0, 0)),
        ],
        out_specs=pl.BlockSpec((1, 1, b, D_MODEL), lambda l, j: (l, j, 0, 0)),
        out_shape=jax.ShapeDtypeStruct((depth, 3, b, D_MODEL), jnp.float32),
        compiler_params=_cparams(2),
        name="adaln_mod",
    )(c, ada_w, ada_b.reshape(depth, 3, 1, D_MODEL))
    return out.reshape(depth, 3, b, 1, D_MODEL)


def _rope_kernel(pos_ref, c_ref, s1_ref, s2_ref):
    pos = pos_ref[0].astype(jnp.float32)
    lane = lax.broadcasted_iota(jnp.int32, (1, LANES), 1)
    cc = lane % HEAD_DIM
    fidx = (cc % (ROT_DIM // 2)).astype(jnp.float32)
    theta = jnp.full((1, LANES), ROPE_THETA, jnp.float32)
    inv_freq = jnp.exp(-(fidx * (2.0 / ROT_DIM)) * jnp.log(theta))
    ang = pos * inv_freq
    cos = jnp.cos(ang)
    sin = jnp.sin(ang)
    c_ref[0] = jnp.where(cc < ROT_DIM, cos, 1.0)
    s1_ref[0] = jnp.where(cc < ROT_DIM // 2, -sin, 0.0)
    s2_ref[0] = jnp.where((cc >= ROT_DIM // 2) & (cc < ROT_DIM), sin, 0.0)


def _rope_tables(positions):
    b, s = positions.shape
    spec = pl.BlockSpec((1, s, LANES), lambda i: (i, 0, 0))
    shp = jax.ShapeDtypeStruct((b, s, LANES), jnp.float32)
    return pl.pallas_call(
        _rope_kernel,
        grid=(b,),
        in_specs=[pl.BlockSpec((1, s, 1), lambda i: (i, 0, 0))],
        out_specs=[spec, spec, spec],
        out_shape=[shp, shp, shp],
        compiler_params=_cparams(1),
        name="rope_tables",
    )(positions.reshape(b, s, 1))


def _residual_update(o_ref, wout_ref, x, gate, npost):
    cat = jnp.concatenate([o_ref[0, j] for j in range(N_UNITS)], axis=1)
    y = jnp.dot(cat, wout_ref[...], preferred_element_type=jnp.float32)
    var = jnp.mean(y * y, axis=-1, keepdims=True)
    return x + gate * (y * lax.rsqrt(var + NORM_EPS) * npost)


def _inproj_kernel(*refs, rope_slabs, with_forget, with_residual):
    refs = list(refs)
    if with_residual:
        o_ref, wout_ref, gate_ref, npost_ref = refs[:4]
        refs = refs[4:]
    x_ref, shift_ref, scale_ref, npre_ref, w_ref = refs[:5]
    refs = refs[5:]
    wf_ref = refs.pop(0) if with_forget else None
    c_ref, s1_ref, s2_ref = refs[:3]
    refs = refs[3:]
    xout_ref = refs.pop(0) if with_residual else None
    out_ref = refs.pop(0)
    fa_ref = refs.pop(0) if with_forget else None

    x = x_ref[0]
    if with_residual:
        x = _residual_update(o_ref, wout_ref, x, gate_ref[...], npost_ref[...])
        xout_ref[0] = x
    var = jnp.mean(x * x, axis=-1, keepdims=True)
    h = (x * lax.rsqrt(var + NORM_EPS) * npre_ref[...]) * (1.0 + scale_ref[...]) + shift_ref[...]
    hb = h.astype(jnp.bfloat16)
    cos = c_ref[0]
    s1 = s1_ref[0]
    s2 = s2_ref[0]
    for c in range(N_SLABS * LANES // MXU_COLS):
        r = jnp.dot(hb, w_ref[:, c * MXU_COLS:(c + 1) * MXU_COLS],
                    preferred_element_type=jnp.float32)
        for half in range(MXU_COLS // LANES):
            j = c * (MXU_COLS // LANES) + half
            y = r[:, half * LANES:(half + 1) * LANES]
            if j in rope_slabs:
                y = (y * cos + pltpu.roll(y, LANES - ROT_DIM // 2, 1) * s1
                     + pltpu.roll(y, ROT_DIM // 2, 1) * s2)
            out_ref[0, j] = y.astype(jnp.bfloat16)
    if with_forget:
        fa_ref[0] = jnp.dot(hb, wf_ref[...], preferred_element_type=jnp.float32)


def _in_projection(x, shift, scale, npre, w, wf, rope, rope_slabs, residual=None):
    b, s, d = x.shape
    with_forget = wf is not None
    with_residual = residual is not None
    row = lambda bi, i: (bi, i, 0)
    const2 = lambda bi, i: (0, 0)
    per_batch = pl.BlockSpec((None, 1, d), lambda bi, i: (bi, 0, 0))
    in_specs, args = [], []
    if with_residual:
        o, w_out, gate, npost = residual
        in_specs += [pl.BlockSpec((1, N_UNITS, TM, LANES), lambda bi, i: (bi, 0, i, 0)),
                     pl.BlockSpec(w_out.shape, const2), per_batch, pl.BlockSpec((1, d), const2)]
        args += [o, w_out, gate, npost.reshape(1, d)]
    in_specs += [pl.BlockSpec((1, TM, d), row), per_batch, per_batch,
                 pl.BlockSpec((1, d), const2), pl.BlockSpec(w.shape, const2)]
    args += [x, shift, scale, npre.reshape(1, d), w]
    if with_forget:
        in_specs.append(pl.BlockSpec(wf.shape, const2))
        args.append(wf)
    in_specs += [pl.BlockSpec((1, TM, LANES), row)] * 3
    args += list(rope)
    out_specs, out_shape = [], []
    if with_residual:
        out_specs.append(pl.BlockSpec((1, TM, d), row))
        out_shape.append(jax.ShapeDtypeStruct((b, s, d), jnp.float32))
    out_specs.append(pl.BlockSpec((1, N_SLABS, TM, LANES), lambda bi, i: (bi, 0, i, 0)))
    out_shape.append(jax.ShapeDtypeStruct((b, N_SLABS, s, LANES), jnp.bfloat16))
    if with_forget:
        out_specs.append(pl.BlockSpec((1, TM, LANES), row))
        out_shape.append(jax.ShapeDtypeStruct((b, s, LANES), jnp.float32))
    res = list(pl.pallas_call(
        functools.partial(_inproj_kernel, rope_slabs=frozenset(rope_slabs),
                          with_forget=with_forget, with_residual=with_residual),
        grid=(b, s // TM),
        in_specs=in_specs,
        out_specs=out_specs,
        out_shape=out_shape,
        compiler_params=_cparams(2),
        name="in_projection",
    )(*args))
    x_new = res.pop(0) if with_residual else x
    proj = res.pop(0)
    fa = res.pop(0) if with_forget else None
    return proj, fa, x_new


F_BLOCK = 256


def _fgate_kernel(fa_ref, b_ref, g_ref):
    s = fa_ref.shape[1]
    z = fa_ref[0] + b_ref[...]
    logf = jnp.minimum(z, 0.0) - jnp.log(1.0 + jnp.exp(-jnp.abs(z)))
    r_i = lax.broadcasted_iota(jnp.int32, (F_BLOCK, F_BLOCK), 0)
    c_i = lax.broadcasted_iota(jnp.int32, (F_BLOCK, F_BLOCK), 1)
    tri = jnp.where(c_i <= r_i, 1.0, 0.0).astype(jnp.bfloat16)
    lane = lax.broadcasted_iota(jnp.int32, (1, LANES), 1)

    def split3(v):
        hi = v.astype(jnp.bfloat16)
        r1 = v - hi.astype(jnp.float32)
        mid = r1.astype(jnp.bfloat16)
        lo = (r1 - mid.astype(jnp.float32)).astype(jnp.bfloat16)
        return hi, mid, lo

    carry = jnp.zeros((1, LANES), jnp.float32)
    for i in range(s // F_BLOCK):
        blk = logf[i * F_BLOCK:(i + 1) * F_BLOCK]
        parts = jnp.concatenate(split3(blk), axis=1)
        cs3 = jnp.dot(tri, parts, preferred_element_type=jnp.float32)
        f = (cs3[:, :LANES] + cs3[:, LANES:2 * LANES]) + cs3[:, 2 * LANES:] + carry
        carry = f[F_BLOCK - 1:F_BLOCK, :]
        hi, mid, lo = split3(f * LOG2E)
        packed = (jnp.where(lane < N_HEADS_FOX, hi.astype(jnp.float32), 0.0)
                  + pltpu.roll(jnp.where(lane < N_HEADS_FOX, mid.astype(jnp.float32), 0.0),
                               N_HEADS_FOX, 1)
                  + pltpu.roll(jnp.where(lane < N_HEADS_FOX, lo.astype(jnp.float32), 0.0),
                               2 * N_HEADS_FOX, 1))
        g_ref[0, i * F_BLOCK:(i + 1) * F_BLOCK, :] = packed.astype(jnp.bfloat16)


def _forget_prefix(fa, b_forget):
    b, s, _ = fa.shape
    bpad = jnp.zeros((1, LANES), jnp.float32).at[0, :N_HEADS_FOX].set(b_forget)
    return pl.pallas_call(
        _fgate_kernel,
        grid=(b,),
        in_specs=[pl.BlockSpec((1, s, LANES), lambda i: (i, 0, 0)),
                  pl.BlockSpec((1, LANES), lambda i: (0, 0))],
        out_specs=pl.BlockSpec((1, s, LANES), lambda i: (i, 0, 0)),
        out_shape=jax.ShapeDtypeStruct((b, s, LANES), jnp.bfloat16),
        compiler_params=_cparams(1),
        name="forget_prefix",
    )(fa, bpad)


V_ROWS = LANES + 16


KPQ = TQ // TK


def _n_bias_tables(mode, s_len):
    if mode == "even":
        return KPQ
    bounded = max((w // d) * d for (w, d) in DILATED_CONFIGS if (w // d) * d < s_len - 1)
    dist = 0
    while dist * TK - (TK - 1) <= bounded:
        dist += 1
    return dist + KPQ


def _bias_tables(mode, s_len):
    kr = np.arange(TK)[:, None]
    qc = np.arange(TQ)[None, :]
    tabs = []
    n_tab = _n_bias_tables(mode, s_len)
    for idx in range(n_tab):
        delta = (idx - (KPQ - 1)) * TK + qc - kr
        off = idx
        if mode == "even":
            mult = (delta >= 0).astype(np.float64)
        else:
            mult = np.zeros((TK, TQ))
            for (w, d) in DILATED_CONFIGS:
                ok = (delta >= 0) & (delta % d == 0) & (delta <= (w // d) * d)
                if off == n_tab - 1:
                    ok = ok & ((w // d) * d >= s_len - 1)
                mult += ok
        tabs.append(np.where(mult > 0, np.log2(np.maximum(mult, 1.0)), NEG))
    return jnp.asarray(np.stack(tabs), jnp.float32)


def _attn_kernel(*refs, mode, lam_init):
    if mode == "even":
        (q_ref, k_ref, v_ref, g_ref, aug_ref, tab_ref, par_ref, o_ref,
         kcat, vt, sb00, sb01, sb10, sb11, cm_buf, m1, a1, m2, a2) = refs
    else:
        (q_ref, k_ref, v_ref, g_ref, tab_ref, o_ref, vt, sb00, sb01, sb10, sb11, cm_buf, m1, a1, m2, a2) = refs
    s_len = q_ref.shape[2]
    u = pl.program_id(1)
    lane = lax.broadcasted_iota(jnp.int32, (1, LANES), 1)
    lo = lane < HEAD_DIM
    nh = N_HEADS_FOX
    nt = (((1,), (1,)), ((), ()))

    vtt = v_ref[0, 0].astype(jnp.float32).T
    ones_rows = jnp.where(lax.broadcasted_iota(jnp.int32, (V_ROWS - LANES, TK), 0) == 0, 1.0, 0.0)
    for j in range(s_len // TK):
        vt[j, 0:LANES, :] = vtt[:, j * TK:(j + 1) * TK].astype(jnp.bfloat16)
        vt[j, LANES:V_ROWS, :] = ones_rows.astype(jnp.bfloat16)

    if mode == "even":
        is_fox = u < N_UNITS // 2
        fox_f = jnp.where(is_fox, 1.0, 0.0)
        g_all = aug_ref[0].astype(jnp.float32)
        kaug = jnp.where(lane < 3 * nh, -g_all, jnp.where(lane < 6 * nh, 1.0, 0.0)) * fox_f
        kcat[:, :LANES] = k_ref[0, 0]
        kcat[:, LANES:] = kaug.astype(jnp.bfloat16)

    state = ((m1, a1), (m2, a2))
    n_tab = tab_ref.shape[0]

    def make_lhs(qi):
        q0 = qi * TQ
        qs = q_ref[0, 0, q0:q0 + TQ, :].astype(jnp.float32) * (HEAD_DIM ** -0.5 * LOG2E)
        lhs = [jnp.where(lo, qs, 0.0).astype(jnp.bfloat16),
               jnp.where(lo, 0.0, qs).astype(jnp.bfloat16)]
        if mode == "even":
            gq = aug_ref[0, q0:q0 + TQ, :].astype(jnp.float32)
            gr = pltpu.roll(gq, 3 * nh, 1)
            for i in range(2):
                h = 2 * u + i
                sel_k = (lane == h) | (lane == h + nh) | (lane == h + 2 * nh)
                sel_q = (lane == h + 3 * nh) | (lane == h + 4 * nh) | (lane == h + 5 * nh)
                qa = jnp.where(sel_k, 1.0, jnp.where(sel_q, gr, 0.0)) * fox_f
                lhs[i] = jnp.concatenate([lhs[i], qa.astype(jnp.bfloat16)], axis=1)
        return lhs

    half = TK // 2
    assert TQ == TK
    s_bufs = ((sb00, sb01), (sb10, sb11))
    z0 = pl.multiple_of(jnp.minimum(pl.program_id(0), 0), TK)

    def scores_into(slot, lhs, qi, kt):
        k0 = kt * TK
        kref = kcat if mode == "even" else k_ref.at[0, 0]
        dist = qi - kt
        for i in range(2):
            if dist > 0:
                st = lax.dot_general(kref[k0:k0 + TK, :], lhs[i], nt,
                                     preferred_element_type=jnp.float32)
                if mode != "even":
                    st = st + tab_ref[min(dist, n_tab - 1)]
                s_bufs[slot][i][pl.ds(z0, TK), :] = st
                cm_buf[slot, i] = jnp.max(st, axis=0, keepdims=True)
            else:
                sa = lax.dot_general(kref[k0:k0 + half, :], lhs[i], nt,
                                     preferred_element_type=jnp.float32) + tab_ref[0, 0:half, :]
                sb = lax.dot_general(kref[k0 + half:k0 + TK, :], lhs[i][half:, :], nt,
                                     preferred_element_type=jnp.float32) + tab_ref[0, half:, half:]
                s_bufs[slot][i][pl.ds(z0, half), :] = sa
                s_bufs[slot][i][pl.ds(z0 + half, half), half:] = sb
                ca = jnp.max(sa, axis=0, keepdims=True)
                cb = jnp.max(sb, axis=0, keepdims=True)
                cm_buf[slot, i] = jnp.concatenate(
                    [ca[:, :half], jnp.maximum(ca[:, half:], cb)], axis=1)

    def consume(slot, qi, kt):
        for i in range(2):
            m_r, a_r = state[i]
            m_prev = m_r[...]
            m_next = jnp.maximum(m_prev, cm_buf[slot, i])
            alpha = jnp.exp2(m_prev - m_next)
            if qi > kt:
                pt = jnp.exp2(s_bufs[slot][i][pl.ds(z0, TK), :] - m_next).astype(jnp.bfloat16)
                a_r[...] = alpha * a_r[...] + jnp.dot(vt[kt], pt, preferred_element_type=jnp.float32)
            else:
                pa = jnp.exp2(s_bufs[slot][i][pl.ds(z0, half), :] - m_next).astype(jnp.bfloat16)
                pb = jnp.exp2(s_bufs[slot][i][pl.ds(z0 + half, half), half:] - m_next[:, half:]).astype(jnp.bfloat16)
                pva = jnp.dot(vt[kt, :, 0:half], pa, preferred_element_type=jnp.float32)
                pvb = jnp.dot(vt[kt, :, half:], pb, preferred_element_type=jnp.float32)
                a_r[:, 0:half] = alpha[:, :half] * a_r[:, 0:half] + pva[:, :half]
                a_r[:, half:] = alpha[:, half:] * a_r[:, half:] + (pva[:, half:] + pvb)
            m_r[...] = m_next

    def finish(qi):
        q0 = qi * TQ
        outs = []
        for (m_r, a_r) in state:
            acc = a_r[...]
            inv_l = 1.0 / acc[LANES:LANES + 1, :]
            outs.append((acc[0:LANES, :] * inv_l).T)
        o1, o2 = outs
        g = g_ref[0, 0, q0:q0 + TQ, :].astype(jnp.float32)
        gate = g * jax.nn.sigmoid(g)
        out = jnp.where(lo, o1, o2)
        if mode == "even":
            par = par_ref[...]
            lam = (jnp.exp(jnp.sum(par[0:1] * par[1:2], axis=1, keepdims=True))
                   - jnp.exp(jnp.sum(par[2:3] * par[3:4], axis=1, keepdims=True)) + lam_init)
            d = o1 - lam * o2
            var = jnp.mean(d * d, axis=-1, keepdims=True)
            dn = (d * lax.rsqrt(var + NORM_EPS) * par[4:5]) * (1.0 - lam_init)
            out = jnp.where(is_fox, out, dn)
        o_ref[0, 0, q0:q0 + TQ, :] = (out * gate).astype(jnp.bfloat16)

    pairs = [(qi, kt) for qi in range(s_len // TQ) for kt in range(qi + 1)]
    lhs = make_lhs(0)
    scores_into(0, lhs, 0, 0)
    for n, (qi, kt) in enumerate(pairs):
        if kt == 0:
            for (m_r, a_r) in state:
                m_r[...] = jnp.full((1, TQ), NEG, jnp.float32)
                a_r[...] = jnp.zeros((V_ROWS, TQ), jnp.float32)
        if n + 1 < len(pairs):
            nqi, nkt = pairs[n + 1]
            if nkt == 0:
                lhs = make_lhs(nqi)
            scores_into((n + 1) % 2, lhs, nqi, nkt)
        consume(n % 2, qi, kt)
        if kt == qi:
            finish(qi)


def _attention(proj, mode, aug=None, params=None, lam_init=0.0):
    b, _, s, _ = proj.shape
    tabs = _bias_tables(mode, s)

    def slab(kind):
        return pl.BlockSpec((1, 1, s, LANES), lambda bi, u: (bi, kind * N_UNITS + u, 0, 0))

    in_specs = [slab(0), slab(1), slab(2), slab(3)]
    args = [proj, proj, proj, proj]
    scratch = []
    if mode == "even":
        in_specs.append(pl.BlockSpec((1, s, LANES), lambda bi, u: (bi, 0, 0)))
        args.append(aug)
        scratch.append(pltpu.VMEM((s, 2 * LANES), jnp.bfloat16))
    in_specs.append(pl.BlockSpec(tabs.shape, lambda bi, u: (0, 0, 0)))
    args.append(tabs)
    if mode == "even":
        in_specs.append(pl.BlockSpec(params.shape, lambda bi, u: (0, 0)))
        args.append(params)
    scratch.append(pltpu.VMEM((s // TK, V_ROWS, TK), jnp.bfloat16))
    scratch += [pltpu.VMEM((TK, TQ), jnp.float32)] * 4
    scratch.append(pltpu.VMEM((2, 2, 1, TQ), jnp.float32))
    scratch += [pltpu.VMEM((1, TQ), jnp.float32), pltpu.VMEM((V_ROWS, TQ), jnp.float32)] * 2
    return pl.pallas_call(
        functools.partial(_attn_kernel, mode=mode, lam_init=lam_init),
        grid=(b, N_UNITS),
        in_specs=in_specs,
        out_specs=pl.BlockSpec((1, 1, s, LANES), lambda bi, u: (bi, u, 0, 0)),
        out_shape=jax.ShapeDtypeStruct((b, N_UNITS, s, LANES), jnp.bfloat16),
        scratch_shapes=scratch,
        compiler_params=_cparams(2),
        name="attention_" + mode,
    )(*args)


def _outproj_kernel(o_ref, w_ref, x_ref, gate_ref, npost_ref, out_ref):
    out_ref[0] = _residual_update(o_ref, w_ref, x_ref[0], gate_ref[...], npost_ref[...])


def _out_projection(o, w, x, gate, npost):
    b, s, d = x.shape
    return pl.pallas_call(
        _outproj_kernel,
        grid=(b, s // TM),
        in_specs=[
            pl.BlockSpec((1, N_UNITS, TM, LANES), lambda bi, i: (bi, 0, i, 0)),
            pl.BlockSpec(w.shape, lambda bi, i: (0, 0)),
            pl.BlockSpec((1, TM, d), lambda bi, i: (bi, i, 0)),
            pl.BlockSpec((None, 1, d), lambda bi, i: (bi, 0, 0)),
            pl.BlockSpec((1, d), lambda bi, i: (0, 0)),
        ],
        out_specs=pl.BlockSpec((1, TM, d), lambda bi, i: (bi, i, 0)),
        out_shape=jax.ShapeDtypeStruct((b, s, d), jnp.float32),
        compiler_params=_cparams(2),
        name="out_projection",
    )(o, w, x, gate, npost.reshape(1, d))


def _even_weight_layout(w_in):
    wf_, wd = N_HEADS_FOX * HEAD_DIM, 4 * 2 * HEAD_DIM
    cuts = np.cumsum([wf_, wf_, wf_, N_HEADS_FOX, wf_, wd, wd, wd, wd])
    qa, ka, va, fa, ga, qd, kd, vd, gd = jnp.split(w_in, [int(v) for v in cuts[:-1]], axis=1)
    main = jnp.concatenate([qa, qd, ka, kd, va, vd, ga, gd], axis=1).astype(jnp.bfloat16)
    wf = jnp.pad(fa, ((0, 0), (0, LANES - N_HEADS_FOX))).astype(jnp.bfloat16)
    return main, wf


def kernel(x, c, positions, norm_pre, norm_post, ada_w, ada_b, ev_w_in, ev_b_forget,
           ev_lambda_q1, ev_lambda_k1, ev_lambda_q2, ev_lambda_k2, ev_subln, ev_w_out,
           od_w_in, od_w_out):
    depth = ada_w.shape[0]
    mod = _adaln_mod(c, ada_w, ada_b)
    rope = _rope_tables(positions)
    half = N_UNITS // 2
    even_rope = set(range(half, N_UNITS)) | set(range(N_UNITS + half, 2 * N_UNITS))
    odd_rope = set(range(2 * N_UNITS))
    pending = None
    for layer in range(depth):
        shift, scale, gate = mod[layer, 0], mod[layer, 1], mod[layer, 2]
        if layer % 2 == 0:
            i = layer // 2
            w_main, wf = _even_weight_layout(ev_w_in[i])
            proj, fa, x = _in_projection(x, shift, scale, norm_pre[layer], w_main, wf, rope, even_rope, pending)
            aug = _forget_prefix(fa, ev_b_forget[i])
            params = jnp.zeros((8, LANES), jnp.float32)
            params = params.at[0, :HEAD_DIM].set(ev_lambda_q1[i]).at[1, :HEAD_DIM].set(ev_lambda_k1[i])
            params = params.at[2, :HEAD_DIM].set(ev_lambda_q2[i]).at[3, :HEAD_DIM].set(ev_lambda_k2[i])
            params = params.at[4, :].set(ev_subln[i])
            lam_init = 0.8 - 0.6 * math.exp(-0.3 * layer)
            o = _attention(proj, "even", aug=aug, params=params, lam_init=lam_init)
            w_out = ev_w_out[i].astype(jnp.bfloat16)
        else:
            j = layer // 2
            proj, _, x = _in_projection(x, shift, scale, norm_pre[layer],
                                        od_w_in[j].astype(jnp.bfloat16), None, rope, odd_rope, pending)
            o = _attention(proj, "odd")
            w_out = od_w_out[j].astype(jnp.bfloat16)
        pending = (o, w_out, gate, norm_post[layer])
    return _out_projection(pending[0], pending[1], x, pending[2], pending[3])
```

```python
import functools
import math

import numpy as np
import jax
import jax.numpy as jnp
from jax import lax
from jax.experimental import pallas as pl
from jax.experimental.pallas import tpu as pltpu

D_MODEL = 1024
HEAD_DIM = 64
N_HEADS_FOX = 8
ROT_DIM = HEAD_DIM // 4
ROPE_THETA = 500000.0
NORM_EPS = 1e-6
DILATED_CONFIGS = ((128, 1), (512, 4), (2048, 16))

LANES = 128
MXU_COLS = 256
N_SLABS = 32
N_UNITS = 8
UNITS_PER_STEP = 2
TQ = 512
TK = 512
TM = 512
NEG = -1e30
LOG2E = math.log2(math.e)
VMEM_LIMIT = 48 * 1024 * 1024


def _cparams(n_axes):
    return pltpu.CompilerParams(
        dimension_semantics=("arbitrary",) * n_axes, vmem_limit_bytes=VMEM_LIMIT)


def _mod_kernel(c_ref, w_ref, b_ref, o_ref):
    c = c_ref[...]
    cond = c * jax.nn.sigmoid(c)
    for j in range(3):
        o_ref[0, j] = (jnp.dot(cond, w_ref[0, :, j * D_MODEL:(j + 1) * D_MODEL],
                               preferred_element_type=jnp.float32) + b_ref[0, j])


def _adaln_mod(c, ada_w, ada_b):
    depth = ada_w.shape[0]
    b = c.shape[0]
    out = pl.pallas_call(
        _mod_kernel,
        grid=(depth,),
        in_specs=[
            pl.BlockSpec((b, D_MODEL), lambda l: (0, 0)),
            pl.BlockSpec((1, D_MODEL, 3 * D_MODEL), lambda l: (l, 0, 0)),
            pl.BlockSpec((1, 3, 1, D_MODEL), lambda l: (l, 0, 0, 0)),
        ],
        out_specs=pl.BlockSpec((1, 3, b, D_MODEL), lambda l: (l, 0, 0, 0)),
        out_shape=jax.ShapeDtypeStruct((depth, 3, b, D_MODEL), jnp.float32),
        compiler_params=_cparams(1),
        name="adaln_mod",
    )(c, ada_w, ada_b.reshape(depth, 3, 1, D_MODEL))
    return out.reshape(depth, 3, b, 1, D_MODEL)


ROPE_PACK = LANES // (ROT_DIM // 2)


def _rope_kernel(pos_ref, c_ref, s1_ref, s2_ref):
    nf = ROT_DIM // 2
    pos = pos_ref[0].astype(jnp.float32)
    lane = lax.broadcasted_iota(jnp.int32, (1, LANES), 1)
    fidx = (lane % nf).astype(jnp.float32)
    theta = jnp.full((1, LANES), ROPE_THETA, jnp.float32)
    inv_freq = jnp.exp(-(fidx * (2.0 / ROT_DIM)) * jnp.log(theta))
    ang = pos * inv_freq
    n_rows = pos.shape[0]

    def split3(v):
        hi = v.astype(jnp.bfloat16)
        r1 = v - hi.astype(jnp.float32)
        mid = r1.astype(jnp.bfloat16)
        lo = (r1 - mid.astype(jnp.float32)).astype(jnp.bfloat16)
        return jnp.concatenate([hi, mid, lo], axis=0)

    cos3 = split3(jnp.cos(ang))
    sin3 = split3(jnp.sin(ang))
    src = lax.broadcasted_iota(jnp.int32, (LANES, LANES), 0)
    dst = lax.broadcasted_iota(jnp.int32, (LANES, LANES), 1)
    cc = dst % HEAD_DIM
    for j in range(ROPE_PACK):
        hit = (src // nf == j) & (src % nf == cc % nf)
        e_c = jnp.where(hit & (cc < ROT_DIM), 1.0, 0.0).astype(jnp.bfloat16)
        e_1 = jnp.where(hit & (cc < nf), -1.0, 0.0).astype(jnp.bfloat16)
        e_2 = jnp.where(hit & (cc >= nf) & (cc < ROT_DIM), 1.0, 0.0).astype(jnp.bfloat16)
        rows = pl.ds(j, n_rows, stride=ROPE_PACK)
        for ref, tab3, e, base in ((c_ref, cos3, e_c, 1.0), (s1_ref, sin3, e_1, 0.0), (s2_ref, sin3, e_2, 0.0)):
            t3 = jnp.dot(tab3, e, preferred_element_type=jnp.float32)
            t = (t3[0:n_rows] + t3[n_rows:2 * n_rows]) + t3[2 * n_rows:]
            if base:
                t = t + jnp.where(dst[0:1] % HEAD_DIM < ROT_DIM, 0.0, base)
            ref[0, rows, :] = t


def _rope_tables(positions):
    b, s = positions.shape
    packed = jnp.repeat(positions.reshape(b, s // ROPE_PACK, ROPE_PACK), ROT_DIM // 2, axis=-1)
    spec = pl.BlockSpec((1, s, LANES), lambda i: (i, 0, 0))
    shp = jax.ShapeDtypeStruct((b, s, LANES), jnp.float32)
    return pl.pallas_call(
        _rope_kernel,
        grid=(b,),
        in_specs=[pl.BlockSpec((1, s // ROPE_PACK, LANES), lambda i: (i, 0, 0))],
        out_specs=[spec, spec, spec],
        out_shape=[shp, shp, shp],
        compiler_params=_cparams(1),
        name="rope_tables",
    )(packed)


def _residual_update(o_ref, wout_ref, x, gate, npost):
    cat = jnp.concatenate([o_ref[0, j] for j in range(N_UNITS)], axis=1)
    y = jnp.dot(cat, wout_ref[...], preferred_element_type=jnp.float32)
    var = jnp.mean(y * y, axis=-1, keepdims=True)
    return x + gate * (y * lax.rsqrt(var + NORM_EPS) * npost)


def _inproj_kernel(*refs, rope_slabs, with_forget, with_residual):
    refs = list(refs)
    if with_residual:
        o_ref, wout_ref, gate_ref, npost_ref = refs[:4]
        refs = refs[4:]
    x_ref, shift_ref, scale_ref, npre_ref, w_ref = refs[:5]
    refs = refs[5:]
    wf_ref = refs.pop(0) if with_forget else None
    c_ref, s1_ref, s2_ref = refs[:3]
    refs = refs[3:]
    xout_ref = refs.pop(0) if with_residual else None
    out_ref = refs.pop(0)
    fa_ref = refs.pop(0) if with_forget else None

    x = x_ref[0]
    if with_residual:
        x = _residual_update(o_ref, wout_ref, x, gate_ref[...], npost_ref[...])
        xout_ref[0] = x
    var = jnp.mean(x * x, axis=-1, keepdims=True)
    h = (x * lax.rsqrt(var + NORM_EPS) * npre_ref[...]) * (1.0 + scale_ref[...]) + shift_ref[...]
    hb = h.astype(jnp.bfloat16)
    cos = c_ref[0]
    s1 = s1_ref[0]
    s2 = s2_ref[0]
    for c in range(N_SLABS * LANES // MXU_COLS):
        r = jnp.dot(hb, w_ref[:, c * MXU_COLS:(c + 1) * MXU_COLS],
                    preferred_element_type=jnp.float32)
        for half in range(MXU_COLS // LANES):
            j = c * (MXU_COLS // LANES) + half
            y = r[:, half * LANES:(half + 1) * LANES]
            if j in rope_slabs:
                y = (y * cos + pltpu.roll(y, LANES - ROT_DIM // 2, 1) * s1
                     + pltpu.roll(y, ROT_DIM // 2, 1) * s2)
            out_ref[0, j] = y.astype(jnp.bfloat16)
    if with_forget:
        fa_ref[0] = jnp.dot(hb, wf_ref[...], preferred_element_type=jnp.float32)


def _in_projection(x, shift, scale, npre, w, wf, rope, rope_slabs, residual=None):
    b, s, d = x.shape
    with_forget = wf is not None
    with_residual = residual is not None
    row = lambda bi, i: (bi, i, 0)
    const2 = lambda bi, i: (0, 0)
    per_batch = pl.BlockSpec((None, 1, d), lambda bi, i: (bi, 0, 0))
    in_specs, args = [], []
    if with_residual:
        o, w_out, gate, npost = residual
        in_specs += [pl.BlockSpec((1, N_UNITS, TM, LANES), lambda bi, i: (bi, 0, i, 0)),
                     pl.BlockSpec(w_out.shape, const2), per_batch, pl.BlockSpec((1, d), const2)]
        args += [o, w_out, gate, npost.reshape(1, d)]
    in_specs += [pl.BlockSpec((1, TM, d), row), per_batch, per_batch,
                 pl.BlockSpec((1, d), const2), pl.BlockSpec(w.shape, const2)]
    args += [x, shift, scale, npre.reshape(1, d), w]
    if with_forget:
        in_specs.append(pl.BlockSpec(wf.shape, const2))
        args.append(wf)
    in_specs += [pl.BlockSpec((1, TM, LANES), row)] * 3
    args += list(rope)
    out_specs, out_shape = [], []
    if with_residual:
        out_specs.append(pl.BlockSpec((1, TM, d), row))
        out_shape.append(jax.ShapeDtypeStruct((b, s, d), jnp.float32))
    out_specs.append(pl.BlockSpec((1, N_SLABS, TM, LANES), lambda bi, i: (bi, 0, i, 0)))
    out_shape.append(jax.ShapeDtypeStruct((b, N_SLABS, s, LANES), jnp.bfloat16))
    if with_forget:
        out_specs.append(pl.BlockSpec((1, TM, LANES), row))
        out_shape.append(jax.ShapeDtypeStruct((b, s, LANES), jnp.float32))
    res = list(pl.pallas_call(
        functools.partial(_inproj_kernel, rope_slabs=frozenset(rope_slabs),
                          with_forget=with_forget, with_residual=with_residual),
        grid=(b, s // TM),
        in_specs=in_specs,
        out_specs=out_specs,
        out_shape=out_shape,
        compiler_params=_cparams(2),
        name="in_projection",
    )(*args))
    x_new = res.pop(0) if with_residual else x
    proj = res.pop(0)
    fa = res.pop(0) if with_forget else None
    return proj, fa, x_new


F_BLOCK = 256


def _fgate_kernel(fa_ref, b_ref, g_ref):
    s = fa_ref.shape[1]
    nh = N_HEADS_FOX
    z = fa_ref[0] + b_ref[...]
    logf = jnp.minimum(z, 0.0) - jnp.log(1.0 + jnp.exp(-jnp.abs(z)))
    r_i = lax.broadcasted_iota(jnp.int32, (F_BLOCK, F_BLOCK), 0)
    c_i = lax.broadcasted_iota(jnp.int32, (F_BLOCK, F_BLOCK), 1)
    tri = jnp.where(c_i <= r_i, 1.0, 0.0).astype(jnp.bfloat16)
    lane = lax.broadcasted_iota(jnp.int32, (1, LANES), 1)

    def split3(v):
        hi = v.astype(jnp.bfloat16)
        r1 = v - hi.astype(jnp.float32)
        mid = r1.astype(jnp.bfloat16)
        lo = (r1 - mid.astype(jnp.float32)).astype(jnp.bfloat16)
        return hi, mid, lo

    carry = jnp.zeros((1, LANES), jnp.float32)
    for i in range(s // F_BLOCK):
        rows = slice(i * F_BLOCK, (i + 1) * F_BLOCK)
        parts = jnp.concatenate(split3(logf[rows]), axis=1)
        cs3 = jnp.dot(tri, parts, preferred_element_type=jnp.float32)
        f = (cs3[:, :LANES] + cs3[:, LANES:2 * LANES]) + cs3[:, 2 * LANES:] + carry
        carry = f[F_BLOCK - 1:F_BLOCK, :]
        hi, mid, lo = split3(f * LOG2E)
        packed = (jnp.where(lane < nh, hi.astype(jnp.float32), 0.0)
                  + pltpu.roll(jnp.where(lane < nh, mid.astype(jnp.float32), 0.0), nh, 1)
                  + pltpu.roll(jnp.where(lane < nh, lo.astype(jnp.float32), 0.0), 2 * nh, 1))
        ones_hi = jnp.where((lane >= 3 * nh) & (lane < 6 * nh), 1.0, 0.0)
        ones_lo = jnp.where(lane < 3 * nh, 1.0, 0.0)
        g_ref[0, 0, 0, rows, :] = (ones_hi - packed).astype(jnp.bfloat16)
        g_ref[0, 0, 1, rows, :] = (ones_lo + pltpu.roll(packed, 3 * nh, 1)).astype(jnp.bfloat16)
    g_ref[0, 1] = jnp.zeros((2, s, LANES), jnp.bfloat16)


def _forget_prefix(fa, b_forget):
    b, s, _ = fa.shape
    bpad = jnp.zeros((1, LANES), jnp.float32).at[0, :N_HEADS_FOX].set(b_forget)
    return pl.pallas_call(
        _fgate_kernel,
        grid=(b,),
        in_specs=[pl.BlockSpec((1, s, LANES), lambda i: (i, 0, 0)),
                  pl.BlockSpec((1, LANES), lambda i: (0, 0))],
        out_specs=pl.BlockSpec((1, 2, 2, s, LANES), lambda i: (i, 0, 0, 0, 0)),
        out_shape=jax.ShapeDtypeStruct((b, 2, 2, s, LANES), jnp.bfloat16),
        compiler_params=_cparams(1),
        name="forget_prefix",
    )(fa, bpad)


V_ROWS = LANES + 16


KPQ = TQ // TK


def _n_bias_tables(mode, s_len):
    if mode == "even":
        return KPQ
    bounded = max((w // d) * d for (w, d) in DILATED_CONFIGS if (w // d) * d < s_len - 1)
    dist = 0
    while dist * TK - (TK - 1) <= bounded:
        dist += 1
    return dist + KPQ


def _bias_tables(mode, s_len):
    kr = np.arange(TK)[:, None]
    qc = np.arange(TQ)[None, :]
    tabs = []
    n_tab = _n_bias_tables(mode, s_len)
    for idx in range(n_tab):
        delta = (idx - (KPQ - 1)) * TK + qc - kr
        off = idx
        if mode == "even":
            mult = (delta >= 0).astype(np.float64)
        else:
            mult = np.zeros((TK, TQ))
            for (w, d) in DILATED_CONFIGS:
                ok = (delta >= 0) & (delta % d == 0) & (delta <= (w // d) * d)
                if off == n_tab - 1:
                    ok = ok & ((w // d) * d >= s_len - 1)
                mult += ok
        tabs.append(np.where(mult > 0, np.log2(np.maximum(mult, 1.0)), NEG))
    return jnp.asarray(np.stack(tabs), jnp.float32)


def _attn_kernel(*refs, mode, lam_init):
    if mode == "even":
        (q_ref, k_ref, v_ref, g_ref, aug_ref, tab_ref, par_ref, o_ref,
         vt, sb00, sb01, sb10, sb11, cm_buf, m1, a1, m2, a2) = refs
    else:
        (q_ref, k_ref, v_ref, g_ref, tab_ref, o_ref, vt, sb00, sb01, sb10, sb11, cm_buf, m1, a1, m2, a2) = refs
    s_len = q_ref.shape[2]

    def unit(uu, carry):
        u = pl.program_id(1) * UNITS_PER_STEP + uu
        lane = lax.broadcasted_iota(jnp.int32, (1, LANES), 1)
        lo = lane < HEAD_DIM
        nh = N_HEADS_FOX
        nt = (((1,), (1,)), ((), ()))

        vtt = v_ref[0, uu].astype(jnp.float32).T
        ones_rows = jnp.where(lax.broadcasted_iota(jnp.int32, (V_ROWS - LANES, TK), 0) == 0, 1.0, 0.0)
        for j in range(s_len // TK):
            vt[j, 0:LANES, :] = vtt[:, j * TK:(j + 1) * TK].astype(jnp.bfloat16)
            vt[j, LANES:V_ROWS, :] = ones_rows.astype(jnp.bfloat16)

        if mode == "even":
            is_fox = u < N_UNITS // 2

        state = ((m1, a1), (m2, a2))
        n_tab = tab_ref.shape[0]

        def make_lhs(qi):
            q0 = qi * TQ
            qs = q_ref[0, uu, q0:q0 + TQ, :].astype(jnp.float32) * (HEAD_DIM ** -0.5 * LOG2E)
            lhs = [jnp.where(lo, qs, 0.0).astype(jnp.bfloat16),
                   jnp.where(lo, 0.0, qs).astype(jnp.bfloat16)]
            if mode == "even":
                gq = aug_ref[0, 0, 1, q0:q0 + TQ, :].astype(jnp.float32)
                for i in range(2):
                    h = 2 * u + i
                    sel = (lane % nh == h) & (lane < 6 * nh)
                    qa = jnp.where(sel, gq, 0.0)
                    lhs[i] = jnp.concatenate([lhs[i], qa.astype(jnp.bfloat16)], axis=1)
            return lhs

        half = TK // 2
        assert TQ == TK
        s_bufs = ((sb00, sb01), (sb10, sb11))
        z0 = pl.multiple_of(jnp.minimum(pl.program_id(0), 0), TK)

        def scores_into(slot, lhs, qi, kt):
            k0 = kt * TK

            def keys(r0, n):
                kb = k_ref[0, uu, r0:r0 + n, :]
                if mode == "even":
                    kb = jnp.concatenate([kb, aug_ref[0, 0, 0, r0:r0 + n, :]], axis=1)
                return kb

            dist = qi - kt
            for i in range(2):
                if dist > 0:
                    st = lax.dot_general(keys(k0, TK), lhs[i], nt,
                                         preferred_element_type=jnp.float32)
                    if mode != "even":
                        st = st + tab_ref[min(dist, n_tab - 1)]
                    s_bufs[slot][i][pl.ds(z0, TK), :] = st
                    cm_buf[slot, i] = jnp.max(st, axis=0, keepdims=True)
                else:
                    sa = lax.dot_general(keys(k0, half), lhs[i], nt,
                                         preferred_element_type=jnp.float32) + tab_ref[0, 0:half, :]
                    sb = lax.dot_general(keys(k0 + half, half), lhs[i][half:, :], nt,
                                         preferred_element_type=jnp.float32) + tab_ref[0, half:, half:]
                    s_bufs[slot][i][pl.ds(z0, half), :] = sa
                    s_bufs[slot][i][pl.ds(z0 + half, half), half:] = sb
                    ca = jnp.max(sa, axis=0, keepdims=True)
                    cb = jnp.max(sb, axis=0, keepdims=True)
                    cm_buf[slot, i] = jnp.concatenate(
                        [ca[:, :half], jnp.maximum(ca[:, half:], cb)], axis=1)

        def consume(slot, qi, kt):
            for i in range(2):
                m_r, a_r = state[i]
                m_prev = m_r[...]
                m_next = jnp.maximum(m_prev, cm_buf[slot, i])
                alpha = jnp.exp2(m_prev - m_next)
                if qi > kt:
                    pt = jnp.exp2(s_bufs[slot][i][pl.ds(z0, TK), :] - m_next).astype(jnp.bfloat16)
                    a_r[...] = alpha * a_r[...] + jnp.dot(vt[kt], pt, preferred_element_type=jnp.float32)
                else:
                    pa = jnp.exp2(s_bufs[slot][i][pl.ds(z0, half), :] - m_next).astype(jnp.bfloat16)
                    pb = jnp.exp2(s_bufs[slot][i][pl.ds(z0 + half, half), half:]
                                  - m_next[:, half:]).astype(jnp.bfloat16)
                    pva = jnp.dot(vt[kt, :, 0:half], pa, preferred_element_type=jnp.float32)
                    pvb = jnp.dot(vt[kt, :, half:], pb, preferred_element_type=jnp.float32)
                    a_r[:, 0:half] = alpha[:, :half] * a_r[:, 0:half] + pva[:, :half]
                    a_r[:, half:] = alpha[:, half:] * a_r[:, half:] + (pva[:, half:] + pvb)
                m_r[...] = m_next

        def finish(qi):
            q0 = qi * TQ
            outs = []
            for (m_r, a_r) in state:
                acc = a_r[...]
                inv_l = 1.0 / acc[LANES:LANES + 1, :]
                outs.append((acc[0:LANES, :] * inv_l).T)
            o1, o2 = outs
            g = g_ref[0, uu, q0:q0 + TQ, :].astype(jnp.float32)
            gate = g * jax.nn.sigmoid(g)
            out = jnp.where(lo, o1, o2)
            if mode == "even":
                par = par_ref[...]
                lam = (jnp.exp(jnp.sum(par[0:1] * par[1:2], axis=1, keepdims=True))
                       - jnp.exp(jnp.sum(par[2:3] * par[3:4], axis=1, keepdims=True)) + lam_init)
                d = o1 - lam * o2
                var = jnp.mean(d * d, axis=-1, keepdims=True)
                dn = (d * lax.rsqrt(var + NORM_EPS) * par[4:5]) * (1.0 - lam_init)
                out = jnp.where(is_fox, out, dn)
            o_ref[0, uu, q0:q0 + TQ, :] = (out * gate).astype(jnp.bfloat16)

        pairs = [(qi, kt) for qi in range(s_len // TQ) for kt in range(qi + 1)]
        lhs = make_lhs(0)
        scores_into(0, lhs, 0, 0)
        for n, (qi, kt) in enumerate(pairs):
            if kt == 0:
                for (m_r, a_r) in state:
                    m_r[...] = jnp.full((1, TQ), NEG, jnp.float32)
                    a_r[...] = jnp.zeros((V_ROWS, TQ), jnp.float32)
            if n + 1 < len(pairs):
                nqi, nkt = pairs[n + 1]
                if nkt == 0:
                    lhs = make_lhs(nqi)
                scores_into((n + 1) % 2, lhs, nqi, nkt)
            consume(n % 2, qi, kt)
            if kt == qi:
                finish(qi)
        return carry

    lax.fori_loop(0, UNITS_PER_STEP, unit, 0)


def _attention(proj, mode, aug=None, params=None, lam_init=0.0):
    b, _, s, _ = proj.shape
    tabs = _bias_tables(mode, s)

    def slab(kind):
        return pl.BlockSpec((1, UNITS_PER_STEP, s, LANES),
                            lambda bi, p: (bi, kind * (N_UNITS // UNITS_PER_STEP) + p, 0, 0))

    in_specs = [slab(0), slab(1), slab(2), slab(3)]
    args = [proj, proj, proj, proj]
    scratch = []
    if mode == "even":
        in_specs.append(pl.BlockSpec((1, 1, 2, s, LANES),
                                     lambda bi, p: (bi, p * UNITS_PER_STEP // (N_UNITS // 2), 0, 0, 0)))
        args.append(aug)
    in_specs.append(pl.BlockSpec(tabs.shape, lambda bi, p: (0, 0, 0)))
    args.append(tabs)
    if mode == "even":
        in_specs.append(pl.BlockSpec(params.shape, lambda bi, p: (0, 0)))
        args.append(params)
    scratch.append(pltpu.VMEM((s // TK, V_ROWS, TK), jnp.bfloat16))
    scratch += [pltpu.VMEM((TK, TQ), jnp.float32)] * 4
    scratch.append(pltpu.VMEM((2, 2, 1, TQ), jnp.float32))
    scratch += [pltpu.VMEM((1, TQ), jnp.float32), pltpu.VMEM((V_ROWS, TQ), jnp.float32)] * 2
    return pl.pallas_call(
        functools.partial(_attn_kernel, mode=mode, lam_init=lam_init),
        grid=(b, N_UNITS // UNITS_PER_STEP),
        in_specs=in_specs,
        out_specs=pl.BlockSpec((1, UNITS_PER_STEP, s, LANES), lambda bi, p: (bi, p, 0, 0)),
        out_shape=jax.ShapeDtypeStruct((b, N_UNITS, s, LANES), jnp.bfloat16),
        scratch_shapes=scratch,
        compiler_params=_cparams(2),
        name="attention_" + mode,
    )(*args)


def _outproj_kernel(o_ref, w_ref, x_ref, gate_ref, npost_ref, out_ref):
    out_ref[0] = _residual_update(o_ref, w_ref, x_ref[0], gate_ref[...], npost_ref[...])


def _out_projection(o, w, x, gate, npost):
    b, s, d = x.shape
    return pl.pallas_call(
        _outproj_kernel,
        grid=(b, s // TM),
        in_specs=[
            pl.BlockSpec((1, N_UNITS, TM, LANES), lambda bi, i: (bi, 0, i, 0)),
            pl.BlockSpec(w.shape, lambda bi, i: (0, 0)),
            pl.BlockSpec((1, TM, d), lambda bi, i: (bi, i, 0)),
            pl.BlockSpec((None, 1, d), lambda bi, i: (bi, 0, 0)),
            pl.BlockSpec((1, d), lambda bi, i: (0, 0)),
        ],
        out_specs=pl.BlockSpec((1, TM, d), lambda bi, i: (bi, i, 0)),
        out_shape=jax.ShapeDtypeStruct((b, s, d), jnp.float32),
        compiler_params=_cparams(2),
        name="out_projection",
    )(o, w, x, gate, npost.reshape(1, d))


def _even_weight_layout(w_in):
    wf_, wd = N_HEADS_FOX * HEAD_DIM, 4 * 2 * HEAD_DIM
    cuts = np.cumsum([wf_, wf_, wf_, N_HEADS_FOX, wf_, wd, wd, wd, wd])
    qa, ka, va, fa, ga, qd, kd, vd, gd = jnp.split(w_in, [int(v) for v in cuts[:-1]], axis=1)
    main = jnp.concatenate([qa, qd, ka, kd, va, vd, ga, gd], axis=1).astype(jnp.bfloat16)
    wf = jnp.pad(fa, ((0, 0), (0, LANES - N_HEADS_FOX))).astype(jnp.bfloat16)
    return main, wf


def kernel(x, c, positions, norm_pre, norm_post, ada_w, ada_b, ev_w_in, ev_b_forget,
           ev_lambda_q1, ev_lambda_k1, ev_lambda_q2, ev_lambda_k2, ev_subln, ev_w_out,
           od_w_in, od_w_out):
    depth = ada_w.shape[0]
    mod = _adaln_mod(c, ada_w, ada_b)
    rope = _rope_tables(positions)
    half = N_UNITS // 2
    even_rope = set(range(half, N_UNITS)) | set(range(N_UNITS + half, 2 * N_UNITS))
    odd_rope = set(range(2 * N_UNITS))
    pending = None
    for layer in range(depth):
        shift, scale, gate = mod[layer, 0], mod[layer, 1], mod[layer, 2]
        if layer % 2 == 0:
            i = layer // 2
            w_main, wf = _even_weight_layout(ev_w_in[i])
            proj, fa, x = _in_projection(x, shift, scale, norm_pre[layer], w_main, wf, rope, even_rope,
                                         pending)
            aug = _forget_prefix(fa, ev_b_forget[i])
            params = jnp.zeros((8, LANES), jnp.float32)
            params = params.at[0, :HEAD_DIM].set(ev_lambda_q1[i]).at[1, :HEAD_DIM].set(ev_lambda_k1[i])
            params = params.at[2, :HEAD_DIM].set(ev_lambda_q2[i]).at[3, :HEAD_DIM].set(ev_lambda_k2[i])
            params = params.at[4, :].set(ev_subln[i])
            lam_init = 0.8 - 0.6 * math.exp(-0.3 * layer)
            o = _attention(proj, "even", aug=aug, params=params, lam_init=lam_init)
            w_out = ev_w_out[i].astype(jnp.bfloat16)
        else:
            j = layer // 2
            proj, _, x = _in_projection(x, shift, scale, norm_pre[layer],
                                        od_w_in[j].astype(jnp.bfloat16), None, rope, odd_rope, pending)
            o = _attention(proj, "odd")
            w_out = od_w_out[j].astype(jnp.bfloat16)
        pending = (o, w_out, gate, norm_post[layer])
    return _out_projection(pending[0], pending[1], x, pending[2], pending[3])
```

```python
import functools
import math

import numpy as np
import jax
import jax.numpy as jnp
from jax import lax
from jax.experimental import pallas as pl
from jax.experimental.pallas import tpu as pltpu

D_MODEL = 1024
HEAD_DIM = 64
N_HEADS_FOX = 8
ROT_DIM = HEAD_DIM // 4
ROPE_THETA = 500000.0
NORM_EPS = 1e-6
DILATED_CONFIGS = ((128, 1), (512, 4), (2048, 16))

LANES = 128
MXU_COLS = 256
N_SLABS = 32
N_UNITS = 8
UNITS_PER_STEP = 1
TQ = 512
TK = 512
TM = 512
NEG = -1e30
LOG2E = math.log2(math.e)
VMEM_LIMIT = 48 * 1024 * 1024


def _cparams(n_axes):
    return pltpu.CompilerParams(
        dimension_semantics=("arbitrary",) * n_axes, vmem_limit_bytes=VMEM_LIMIT)


def _mod_kernel(c_ref, w_ref, b_ref, o_ref):
    c = c_ref[...]
    cond = c * jax.nn.sigmoid(c)
    for j in range(3):
        o_ref[0, j] = (jnp.dot(cond, w_ref[0, :, j * D_MODEL:(j + 1) * D_MODEL],
                               preferred_element_type=jnp.float32) + b_ref[0, j])


def _adaln_mod(c, ada_w, ada_b):
    depth = ada_w.shape[0]
    b = c.shape[0]
    out = pl.pallas_call(
        _mod_kernel,
        grid=(depth,),
        in_specs=[
            pl.BlockSpec((b, D_MODEL), lambda l: (0, 0)),
            pl.BlockSpec((1, D_MODEL, 3 * D_MODEL), lambda l: (l, 0, 0)),
            pl.BlockSpec((1, 3, 1, D_MODEL), lambda l: (l, 0, 0, 0)),
        ],
        out_specs=pl.BlockSpec((1, 3, b, D_MODEL), lambda l: (l, 0, 0, 0)),
        out_shape=jax.ShapeDtypeStruct((depth, 3, b, D_MODEL), jnp.float32),
        compiler_params=_cparams(1),
        name="adaln_mod",
    )(c, ada_w, ada_b.reshape(depth, 3, 1, D_MODEL))
    return out.reshape(depth, 3, b, 1, D_MODEL)


ROPE_PACK = LANES // (ROT_DIM // 2)


def _rope_kernel(pos_ref, c_ref, s1_ref, s2_ref):
    nf = ROT_DIM // 2
    pos = pos_ref[0].astype(jnp.float32)
    lane = lax.broadcasted_iota(jnp.int32, (1, LANES), 1)
    fidx = (lane % nf).astype(jnp.float32)
    theta = jnp.full((1, LANES), ROPE_THETA, jnp.float32)
    inv_freq = jnp.exp(-(fidx * (2.0 / ROT_DIM)) * jnp.log(theta))
    ang = pos * inv_freq
    n_rows = pos.shape[0]

    def split3(v):
        hi = v.astype(jnp.bfloat16)
        r1 = v - hi.astype(jnp.float32)
        mid = r1.astype(jnp.bfloat16)
        lo = (r1 - mid.astype(jnp.float32)).astype(jnp.bfloat16)
        return jnp.concatenate([hi, mid, lo], axis=0)

    cos3 = split3(jnp.cos(ang))
    sin3 = split3(jnp.sin(ang))
    src = lax.broadcasted_iota(jnp.int32, (LANES, LANES), 0)
    dst = lax.broadcasted_iota(jnp.int32, (LANES, LANES), 1)
    cc = dst % HEAD_DIM
    for j in range(ROPE_PACK):
        hit = (src // nf == j) & (src % nf == cc % nf)
        e_c = jnp.where(hit & (cc < ROT_DIM), 1.0, 0.0).astype(jnp.bfloat16)
        e_1 = jnp.where(hit & (cc < nf), -1.0, 0.0).astype(jnp.bfloat16)
        e_2 = jnp.where(hit & (cc >= nf) & (cc < ROT_DIM), 1.0, 0.0).astype(jnp.bfloat16)
        rows = pl.ds(j, n_rows, stride=ROPE_PACK)
        for ref, tab3, e, base in ((c_ref, cos3, e_c, 1.0), (s1_ref, sin3, e_1, 0.0), (s2_ref, sin3, e_2, 0.0)):
            t3 = jnp.dot(tab3, e, preferred_element_type=jnp.float32)
            t = (t3[0:n_rows] + t3[n_rows:2 * n_rows]) + t3[2 * n_rows:]
            if base:
                t = t + jnp.where(dst[0:1] % HEAD_DIM < ROT_DIM, 0.0, base)
            ref[0, rows, :] = t


def _rope_tables(positions):
    b, s = positions.shape
    packed = jnp.repeat(positions.reshape(b, s // ROPE_PACK, ROPE_PACK), ROT_DIM // 2, axis=-1)
    spec = pl.BlockSpec((1, s, LANES), lambda i: (i, 0, 0))
    shp = jax.ShapeDtypeStruct((b, s, LANES), jnp.float32)
    return pl.pallas_call(
        _rope_kernel,
        grid=(b,),
        in_specs=[pl.BlockSpec((1, s // ROPE_PACK, LANES), lambda i: (i, 0, 0))],
        out_specs=[spec, spec, spec],
        out_shape=[shp, shp, shp],
        compiler_params=_cparams(1),
        name="rope_tables",
    )(packed)


def _residual_update(o_ref, wout_ref, x, gate, npost):
    cat = jnp.concatenate([o_ref[0, j] for j in range(N_UNITS)], axis=1)
    y = jnp.dot(cat, wout_ref[...], preferred_element_type=jnp.float32)
    var = jnp.mean(y * y, axis=-1, keepdims=True)
    return x + gate * (y * lax.rsqrt(var + NORM_EPS) * npost)


def _inproj_kernel(*refs, rope_slabs, with_forget, with_residual):
    refs = list(refs)
    if with_residual:
        o_ref, wout_ref, gate_ref, npost_ref = refs[:4]
        refs = refs[4:]
    x_ref, shift_ref, scale_ref, npre_ref, w_ref = refs[:5]
    refs = refs[5:]
    wf_ref = refs.pop(0) if with_forget else None
    c_ref, s1_ref, s2_ref = refs[:3]
    refs = refs[3:]
    xout_ref = refs.pop(0) if with_residual else None
    out_ref = refs.pop(0)
    fa_ref = refs.pop(0) if with_forget else None

    x = x_ref[0]
    if with_residual:
        x = _residual_update(o_ref, wout_ref, x, gate_ref[...], npost_ref[...])
        xout_ref[0] = x
    var = jnp.mean(x * x, axis=-1, keepdims=True)
    h = (x * lax.rsqrt(var + NORM_EPS) * npre_ref[...]) * (1.0 + scale_ref[...]) + shift_ref[...]
    hb = h.astype(jnp.bfloat16)
    cos = c_ref[0]
    s1 = s1_ref[0]
    s2 = s2_ref[0]
    for c in range(N_SLABS * LANES // MXU_COLS):
        r = jnp.dot(hb, w_ref[:, c * MXU_COLS:(c + 1) * MXU_COLS],
                    preferred_element_type=jnp.float32)
        for half in range(MXU_COLS // LANES):
            j = c * (MXU_COLS // LANES) + half
            y = r[:, half * LANES:(half + 1) * LANES]
            if j in rope_slabs:
                y = (y * cos + pltpu.roll(y, LANES - ROT_DIM // 2, 1) * s1
                     + pltpu.roll(y, ROT_DIM // 2, 1) * s2)
            out_ref[0, j] = y.astype(jnp.bfloat16)
    if with_forget:
        fa_ref[0] = jnp.dot(hb, wf_ref[...], preferred_element_type=jnp.float32)


def _mod_spec(layer, which):
    return pl.BlockSpec((None, None, None, 1, D_MODEL), lambda bi, i: (layer, which, bi, 0, 0))


def _layer_spec(shape, index):
    return pl.BlockSpec((None,) + tuple(shape[1:]), lambda bi, i: (index,) + (0,) * (len(shape) - 1))


def _in_projection(x, mod, layer, norm_pre, w, w_idx, wf, rope, rope_slabs, residual=None):
    b, s, d = x.shape
    with_forget = wf is not None
    with_residual = residual is not None
    row = lambda bi, i: (bi, i, 0)
    in_specs, args = [], []
    if with_residual:
        o, w_out, out_idx, norm_post, prev = residual
        in_specs += [pl.BlockSpec((1, N_UNITS, TM, LANES), lambda bi, i: (bi, 0, i, 0)),
                     _layer_spec(w_out.shape, out_idx), _mod_spec(prev, 2), _layer_spec(norm_post.shape, prev)]
        args += [o, w_out, mod, norm_post]
    in_specs += [pl.BlockSpec((1, TM, d), row), _mod_spec(layer, 0), _mod_spec(layer, 1),
                 _layer_spec(norm_pre.shape, layer), _layer_spec(w.shape, w_idx)]
    args += [x, mod, mod, norm_pre, w]
    if with_forget:
        in_specs.append(_layer_spec(wf.shape, w_idx))
        args.append(wf)
    in_specs += [pl.BlockSpec((1, TM, LANES), row)] * 3
    args += list(rope)
    out_specs, out_shape = [], []
    if with_residual:
        out_specs.append(pl.BlockSpec((1, TM, d), row))
        out_shape.append(jax.ShapeDtypeStruct((b, s, d), jnp.float32))
    out_specs.append(pl.BlockSpec((1, N_SLABS, TM, LANES), lambda bi, i: (bi, 0, i, 0)))
    out_shape.append(jax.ShapeDtypeStruct((b, N_SLABS, s, LANES), jnp.bfloat16))
    if with_forget:
        out_specs.append(pl.BlockSpec((1, TM, LANES), row))
        out_shape.append(jax.ShapeDtypeStruct((b, s, LANES), jnp.float32))
    res = list(pl.pallas_call(
        functools.partial(_inproj_kernel, rope_slabs=frozenset(rope_slabs),
                          with_forget=with_forget, with_residual=with_residual),
        grid=(b, s // TM),
        in_specs=in_specs,
        out_specs=out_specs,
        out_shape=out_shape,
        compiler_params=_cparams(2),
        name="in_projection",
    )(*args))
    x_new = res.pop(0) if with_residual else x
    proj = res.pop(0)
    fa = res.pop(0) if with_forget else None
    return proj, fa, x_new


F_BLOCK = 256


def _fgate_kernel(fa_ref, b_ref, g_ref):
    s = fa_ref.shape[1]
    nh = N_HEADS_FOX
    z = fa_ref[0] + b_ref[...]
    logf = jnp.minimum(z, 0.0) - jnp.log(1.0 + jnp.exp(-jnp.abs(z)))
    r_i = lax.broadcasted_iota(jnp.int32, (F_BLOCK, F_BLOCK), 0)
    c_i = lax.broadcasted_iota(jnp.int32, (F_BLOCK, F_BLOCK), 1)
    tri = jnp.where(c_i <= r_i, 1.0, 0.0).astype(jnp.bfloat16)
    lane = lax.broadcasted_iota(jnp.int32, (1, LANES), 1)

    def split3(v):
        hi = v.astype(jnp.bfloat16)
        r1 = v - hi.astype(jnp.float32)
        mid = r1.astype(jnp.bfloat16)
        lo = (r1 - mid.astype(jnp.float32)).astype(jnp.bfloat16)
        return hi, mid, lo

    carry = jnp.zeros((1, LANES), jnp.float32)
    for i in range(s // F_BLOCK):
        rows = slice(i * F_BLOCK, (i + 1) * F_BLOCK)
        parts = jnp.concatenate(split3(logf[rows]), axis=1)
        cs3 = jnp.dot(tri, parts, preferred_element_type=jnp.float32)
        f = (cs3[:, :LANES] + cs3[:, LANES:2 * LANES]) + cs3[:, 2 * LANES:] + carry
        carry = f[F_BLOCK - 1:F_BLOCK, :]
        hi, mid, lo = split3(f * LOG2E)
        packed = (jnp.where(lane < nh, hi.astype(jnp.float32), 0.0)
                  + pltpu.roll(jnp.where(lane < nh, mid.astype(jnp.float32), 0.0), nh, 1)
                  + pltpu.roll(jnp.where(lane < nh, lo.astype(jnp.float32), 0.0), 2 * nh, 1))
        ones_hi = jnp.where((lane >= 3 * nh) & (lane < 6 * nh), 1.0, 0.0)
        ones_lo = jnp.where(lane < 3 * nh, 1.0, 0.0)
        g_ref[0, 0, 0, rows, :] = (ones_hi - packed).astype(jnp.bfloat16)
        g_ref[0, 0, 1, rows, :] = (ones_lo + pltpu.roll(packed, 3 * nh, 1)).astype(jnp.bfloat16)
    g_ref[0, 1] = jnp.zeros((2, s, LANES), jnp.bfloat16)


def _forget_prefix(fa, b_forget, idx):
    b, s, _ = fa.shape
    return pl.pallas_call(
        _fgate_kernel,
        grid=(b,),
        in_specs=[pl.BlockSpec((1, s, LANES), lambda i: (i, 0, 0)),
                  pl.BlockSpec((None, 1, LANES), lambda i: (idx, 0, 0))],
        out_specs=pl.BlockSpec((1, 2, 2, s, LANES), lambda i: (i, 0, 0, 0, 0)),
        out_shape=jax.ShapeDtypeStruct((b, 2, 2, s, LANES), jnp.bfloat16),
        compiler_params=_cparams(1),
        name="forget_prefix",
    )(fa, b_forget)


V_ROWS = LANES + 16


KPQ = TQ // TK


def _n_bias_tables(mode, s_len):
    if mode == "even":
        return KPQ
    bounded = max((w // d) * d for (w, d) in DILATED_CONFIGS if (w // d) * d < s_len - 1)
    dist = 0
    while dist * TK - (TK - 1) <= bounded:
        dist += 1
    return dist + KPQ


def _bias_tables(mode, s_len):
    kr = np.arange(TK)[:, None]
    qc = np.arange(TQ)[None, :]
    tabs = []
    n_tab = _n_bias_tables(mode, s_len)
    for idx in range(n_tab):
        delta = (idx - (KPQ - 1)) * TK + qc - kr
        off = idx
        if mode == "even":
            mult = (delta >= 0).astype(np.float64)
        else:
            mult = np.zeros((TK, TQ))
            for (w, d) in DILATED_CONFIGS:
                ok = (delta >= 0) & (delta % d == 0) & (delta <= (w // d) * d)
                if off == n_tab - 1:
                    ok = ok & ((w // d) * d >= s_len - 1)
                mult += ok
        tabs.append(np.where(mult > 0, np.log2(np.maximum(mult, 1.0)), NEG))
    return jnp.asarray(np.stack(tabs), jnp.float32)


def _attn_kernel(*refs, mode, lam_init, par_idx):
    if mode == "even":
        (q_ref, k_ref, v_ref, g_ref, aug_ref, tab_ref, lq1_ref, lk1_ref, lq2_ref, lk2_ref, subln_ref, o_ref,
         vt, sb00, sb01, sb10, sb11, cm_buf, m1, a1, m2, a2) = refs
    else:
        (q_ref, k_ref, v_ref, g_ref, tab_ref, o_ref, vt, sb00, sb01, sb10, sb11, cm_buf, m1, a1, m2, a2) = refs
    s_len = q_ref.shape[2]

    def unit(uu, carry):
        u = pl.program_id(1) * UNITS_PER_STEP + uu
        lane = lax.broadcasted_iota(jnp.int32, (1, LANES), 1)
        lo = lane < HEAD_DIM
        nh = N_HEADS_FOX
        nt = (((1,), (1,)), ((), ()))

        vtt = v_ref[0, uu].astype(jnp.float32).T
        ones_rows = jnp.where(lax.broadcasted_iota(jnp.int32, (V_ROWS - LANES, TK), 0) == 0, 1.0, 0.0)
        for j in range(s_len // TK):
            vt[j, 0:LANES, :] = vtt[:, j * TK:(j + 1) * TK].astype(jnp.bfloat16)
            vt[j, LANES:V_ROWS, :] = ones_rows.astype(jnp.bfloat16)

        if mode == "even":
            is_fox = u < N_UNITS // 2

        state = ((m1, a1), (m2, a2))
        n_tab = tab_ref.shape[0]

        def make_lhs(qi):
            q0 = qi * TQ
            qs = q_ref[0, uu, q0:q0 + TQ, :].astype(jnp.float32) * (HEAD_DIM ** -0.5 * LOG2E)
            lhs = [jnp.where(lo, qs, 0.0).astype(jnp.bfloat16),
                   jnp.where(lo, 0.0, qs).astype(jnp.bfloat16)]
            if mode == "even":
                gq = aug_ref[0, 0, 1, q0:q0 + TQ, :].astype(jnp.float32)
                for i in range(2):
                    h = 2 * u + i
                    sel = (lane % nh == h) & (lane < 6 * nh)
                    qa = jnp.where(sel, gq, 0.0)
                    lhs[i] = jnp.concatenate([lhs[i], qa.astype(jnp.bfloat16)], axis=1)
            return lhs

        half = TK // 2
        assert TQ == TK
        s_bufs = ((sb00, sb01), (sb10, sb11))
        z0 = pl.multiple_of(jnp.minimum(pl.program_id(0), 0), TK)

        def scores_into(slot, lhs, qi, kt):
            k0 = kt * TK

            def keys(r0, n):
                kb = k_ref[0, uu, r0:r0 + n, :]
                if mode == "even":
                    kb = jnp.concatenate([kb, aug_ref[0, 0, 0, r0:r0 + n, :]], axis=1)
                return kb

            dist = qi - kt
            for i in range(2):
                if dist > 0:
                    st = lax.dot_general(keys(k0, TK), lhs[i], nt,
                                         preferred_element_type=jnp.float32)
                    if mode != "even":
                        st = st + tab_ref[min(dist, n_tab - 1)]
                    s_bufs[slot][i][pl.ds(z0, TK), :] = st
                    cm_buf[slot, i] = jnp.max(st, axis=0, keepdims=True)
                else:
                    sa = lax.dot_general(keys(k0, half), lhs[i], nt,
                                         preferred_element_type=jnp.float32) + tab_ref[0, 0:half, :]
                    sb = lax.dot_general(keys(k0 + half, half), lhs[i][half:, :], nt,
                                         preferred_element_type=jnp.float32) + tab_ref[0, half:, half:]
                    s_bufs[slot][i][pl.ds(z0, half), :] = sa
                    s_bufs[slot][i][pl.ds(z0 + half, half), half:] = sb
                    ca = jnp.max(sa, axis=0, keepdims=True)
                    cb = jnp.max(sb, axis=0, keepdims=True)
                    cm_buf[slot, i] = jnp.concatenate(
                        [ca[:, :half], jnp.maximum(ca[:, half:], cb)], axis=1)

        def consume(slot, qi, kt):
            for i in range(2):
                m_r, a_r = state[i]
                m_prev = m_r[...]
                m_next = jnp.maximum(m_prev, cm_buf[slot, i])
                alpha = jnp.exp2(m_prev - m_next)
                if qi > kt:
                    pt = jnp.exp2(s_bufs[slot][i][pl.ds(z0, TK), :] - m_next).astype(jnp.bfloat16)
                    a_r[...] = alpha * a_r[...] + jnp.dot(vt[kt], pt, preferred_element_type=jnp.float32)
                else:
                    pa = jnp.exp2(s_bufs[slot][i][pl.ds(z0, half), :] - m_next).astype(jnp.bfloat16)
                    pb = jnp.exp2(s_bufs[slot][i][pl.ds(z0 + half, half), half:]
                                  - m_next[:, half:]).astype(jnp.bfloat16)
                    pva = jnp.dot(vt[kt, :, 0:half], pa, preferred_element_type=jnp.float32)
                    pvb = jnp.dot(vt[kt, :, half:], pb, preferred_element_type=jnp.float32)
                    a_r[:, 0:half] = alpha[:, :half] * a_r[:, 0:half] + pva[:, :half]
                    a_r[:, half:] = alpha[:, half:] * a_r[:, half:] + (pva[:, half:] + pvb)
                m_r[...] = m_next

        def finish(qi):
            q0 = qi * TQ
            outs = []
            for (m_r, a_r) in state:
                acc = a_r[...]
                inv_l = 1.0 / acc[LANES:LANES + 1, :]
                outs.append((acc[0:LANES, :] * inv_l).T)
            o1, o2 = outs
            g = g_ref[0, uu, q0:q0 + TQ, :].astype(jnp.float32)
            gate = g * jax.nn.sigmoid(g)
            out = jnp.where(lo, o1, o2)
            if mode == "even":
                pr = slice(par_idx, par_idx + 1)
                lam = (jnp.exp(jnp.sum(lq1_ref[pr, :] * lk1_ref[pr, :], axis=1, keepdims=True))
                       - jnp.exp(jnp.sum(lq2_ref[pr, :] * lk2_ref[pr, :], axis=1, keepdims=True)) + lam_init)
                d = o1 - lam * o2
                var = jnp.mean(d * d, axis=-1, keepdims=True)
                dn = (d * lax.rsqrt(var + NORM_EPS) * subln_ref[pr, :]) * (1.0 - lam_init)
                out = jnp.where(is_fox, out, dn)
            o_ref[0, uu, q0:q0 + TQ, :] = (out * gate).astype(jnp.bfloat16)

        pairs = [(qi, kt) for qi in range(s_len // TQ) for kt in range(qi + 1)]
        lhs = make_lhs(0)
        scores_into(0, lhs, 0, 0)
        for n, (qi, kt) in enumerate(pairs):
            if kt == 0:
                for (m_r, a_r) in state:
                    m_r[...] = jnp.full((1, TQ), NEG, jnp.float32)
                    a_r[...] = jnp.zeros((V_ROWS, TQ), jnp.float32)
            if n + 1 < len(pairs):
                nqi, nkt = pairs[n + 1]
                if nkt == 0:
                    lhs = make_lhs(nqi)
                scores_into((n + 1) % 2, lhs, nqi, nkt)
            consume(n % 2, qi, kt)
            if kt == qi:
                finish(qi)
        return carry

    lax.fori_loop(0, UNITS_PER_STEP, unit, 0)


def _attention(proj, mode, aug=None, params=None, par_idx=0, lam_init=0.0):
    b, _, s, _ = proj.shape
    tabs = _bias_tables(mode, s)

    def slab(kind):
        return pl.BlockSpec((1, UNITS_PER_STEP, s, LANES),
                            lambda bi, p: (bi, kind * (N_UNITS // UNITS_PER_STEP) + p, 0, 0))

    in_specs = [slab(0), slab(1), slab(2), slab(3)]
    args = [proj, proj, proj, proj]
    scratch = []
    if mode == "even":
        in_specs.append(pl.BlockSpec((1, 1, 2, s, LANES),
                                     lambda bi, p: (bi, p * UNITS_PER_STEP // (N_UNITS // 2), 0, 0, 0)))
        args.append(aug)
    in_specs.append(pl.BlockSpec(tabs.shape, lambda bi, p: (0, 0, 0)))
    args.append(tabs)
    if mode == "even":
        for arr in params:
            in_specs.append(pl.BlockSpec(arr.shape, lambda bi, p: (0, 0)))
            args.append(arr)
    scratch.append(pltpu.VMEM((s // TK, V_ROWS, TK), jnp.bfloat16))
    scratch += [pltpu.VMEM((TK, TQ), jnp.float32)] * 4
    scratch.append(pltpu.VMEM((2, 2, 1, TQ), jnp.float32))
    scratch += [pltpu.VMEM((1, TQ), jnp.float32), pltpu.VMEM((V_ROWS, TQ), jnp.float32)] * 2
    return pl.pallas_call(
        functools.partial(_attn_kernel, mode=mode, lam_init=lam_init, par_idx=par_idx),
        grid=(b, N_UNITS // UNITS_PER_STEP),
        in_specs=in_specs,
        out_specs=pl.BlockSpec((1, UNITS_PER_STEP, s, LANES), lambda bi, p: (bi, p, 0, 0)),
        out_shape=jax.ShapeDtypeStruct((b, N_UNITS, s, LANES), jnp.bfloat16),
        scratch_shapes=scratch,
        compiler_params=_cparams(2),
        name="attention_" + mode,
    )(*args)


def _outproj_kernel(o_ref, w_ref, x_ref, gate_ref, npost_ref, out_ref):
    out_ref[0] = _residual_update(o_ref, w_ref, x_ref[0], gate_ref[...], npost_ref[...])


def _out_projection(o, w_out, out_idx, x, mod, layer, norm_post):
    b, s, d = x.shape
    return pl.pallas_call(
        _outproj_kernel,
        grid=(b, s // TM),
        in_specs=[
            pl.BlockSpec((1, N_UNITS, TM, LANES), lambda bi, i: (bi, 0, i, 0)),
            _layer_spec(w_out.shape, out_idx),
            pl.BlockSpec((1, TM, d), lambda bi, i: (bi, i, 0)),
            _mod_spec(layer, 2),
            _layer_spec(norm_post.shape, layer),
        ],
        out_specs=pl.BlockSpec((1, TM, d), lambda bi, i: (bi, i, 0)),
        out_shape=jax.ShapeDtypeStruct((b, s, d), jnp.float32),
        compiler_params=_cparams(2),
        name="out_projection",
    )(o, w_out, x, mod, norm_post)


def _even_weight_layout(w_in):
    wf_, wd = N_HEADS_FOX * HEAD_DIM, 4 * 2 * HEAD_DIM
    cuts = np.cumsum([wf_, wf_, wf_, N_HEADS_FOX, wf_, wd, wd, wd, wd])
    qa, ka, va, fa, ga, qd, kd, vd, gd = jnp.split(w_in, [int(v) for v in cuts[:-1]], axis=2)
    main = jnp.concatenate([qa, qd, ka, kd, va, vd, ga, gd], axis=2).astype(jnp.bfloat16)
    wf = jnp.pad(fa, ((0, 0), (0, 0), (0, LANES - N_HEADS_FOX))).astype(jnp.bfloat16)
    return main, wf


def kernel(x, c, positions, norm_pre, norm_post, ada_w, ada_b, ev_w_in, ev_b_forget,
           ev_lambda_q1, ev_lambda_k1, ev_lambda_q2, ev_lambda_k2, ev_subln, ev_w_out,
           od_w_in, od_w_out):
    depth, d = norm_pre.shape
    mod = _adaln_mod(c, ada_w, ada_b)
    rope = _rope_tables(positions)
    half = N_UNITS // 2
    even_rope = set(range(half, N_UNITS)) | set(range(N_UNITS + half, 2 * N_UNITS))
    odd_rope = set(range(2 * N_UNITS))
    ev_w_main, ev_wf = _even_weight_layout(ev_w_in)
    od_w_main = od_w_in.astype(jnp.bfloat16)
    w_outs = (ev_w_out.astype(jnp.bfloat16), od_w_out.astype(jnp.bfloat16))
    npre = norm_pre.reshape(depth, 1, d)
    npost = norm_post.reshape(depth, 1, d)
    b_forget = jnp.pad(ev_b_forget, ((0, 0), (0, LANES - N_HEADS_FOX))).reshape(-1, 1, LANES)
    lam_params = (ev_lambda_q1, ev_lambda_k1, ev_lambda_q2, ev_lambda_k2, ev_subln)
    pending = None
    for layer in range(depth):
        idx = layer // 2
        if layer % 2 == 0:
            proj, fa, x = _in_projection(x, mod, layer, npre, ev_w_main, idx, ev_wf, rope, even_rope, pending)
            aug = _forget_prefix(fa, b_forget, idx)
            lam_init = 0.8 - 0.6 * math.exp(-0.3 * layer)
            o = _attention(proj, "even", aug=aug, params=lam_params, par_idx=idx, lam_init=lam_init)
        else:
            proj, _, x = _in_projection(x, mod, layer, npre, od_w_main, idx, None, rope, odd_rope, pending)
            o = _attention(proj, "odd")
        pending = (o, w_outs[layer % 2], idx, npost, layer)
    o, w_out, idx, _, layer = pending
    return _out_projection(o, w_out, idx, x, mod, layer, npost)
```

```python
import functools
import math

import numpy as np
import jax
import jax.numpy as jnp
from jax import lax
from jax.experimental import pallas as pl
from jax.experimental.pallas import tpu as pltpu

D_MODEL = 1024
HEAD_DIM = 64
N_HEADS_FOX = 8
ROT_DIM = HEAD_DIM // 4
ROPE_THETA = 500000.0
NORM_EPS = 1e-6
DILATED_CONFIGS = ((128, 1), (512, 4), (2048, 16))

LANES = 128
MXU_COLS = 256
N_SLABS = 32
N_UNITS = 8
UNITS_PER_STEP = 1
TQ = 512
TK = 512
TM = 512
NEG = -1e30
LOG2E = math.log2(math.e)
VMEM_LIMIT = 48 * 1024 * 1024


def _cparams(n_axes):
    return pltpu.CompilerParams(
        dimension_semantics=("arbitrary",) * n_axes, vmem_limit_bytes=VMEM_LIMIT)


def _mod_kernel(c_ref, w_ref, b_ref, o_ref):
    c = c_ref[...]
    cond = c * jax.nn.sigmoid(c)
    for j in range(3):
        o_ref[0, j] = (jnp.dot(cond, w_ref[0, :, j * D_MODEL:(j + 1) * D_MODEL],
                               preferred_element_type=jnp.float32) + b_ref[0, j])


def _adaln_mod(c, ada_w, ada_b):
    depth = ada_w.shape[0]
    b = c.shape[0]
    out = pl.pallas_call(
        _mod_kernel,
        grid=(depth,),
        in_specs=[
            pl.BlockSpec((b, D_MODEL), lambda l: (0, 0)),
            pl.BlockSpec((1, D_MODEL, 3 * D_MODEL), lambda l: (l, 0, 0)),
            pl.BlockSpec((1, 3, 1, D_MODEL), lambda l: (l, 0, 0, 0)),
        ],
        out_specs=pl.BlockSpec((1, 3, b, D_MODEL), lambda l: (l, 0, 0, 0)),
        out_shape=jax.ShapeDtypeStruct((depth, 3, b, D_MODEL), jnp.float32),
        compiler_params=_cparams(1),
        name="adaln_mod",
    )(c, ada_w, ada_b.reshape(depth, 3, 1, D_MODEL))
    return out.reshape(depth, 3, b, 1, D_MODEL)


ROPE_PACK = LANES // (ROT_DIM // 2)


def _rope_kernel(pos_ref, c_ref, s1_ref, s2_ref):
    nf = ROT_DIM // 2
    pos = pos_ref[0].astype(jnp.float32)
    lane = lax.broadcasted_iota(jnp.int32, (1, LANES), 1)
    fidx = (lane % nf).astype(jnp.float32)
    theta = jnp.full((1, LANES), ROPE_THETA, jnp.float32)
    inv_freq = jnp.exp(-(fidx * (2.0 / ROT_DIM)) * jnp.log(theta))
    ang = pos * inv_freq
    n_rows = pos.shape[0]

    def split3(v):
        hi = v.astype(jnp.bfloat16)
        r1 = v - hi.astype(jnp.float32)
        mid = r1.astype(jnp.bfloat16)
        lo = (r1 - mid.astype(jnp.float32)).astype(jnp.bfloat16)
        return jnp.concatenate([hi, mid, lo], axis=0)

    cos3 = split3(jnp.cos(ang))
    sin3 = split3(jnp.sin(ang))
    src = lax.broadcasted_iota(jnp.int32, (LANES, LANES), 0)
    dst = lax.broadcasted_iota(jnp.int32, (LANES, LANES), 1)
    cc = dst % HEAD_DIM
    for j in range(ROPE_PACK):
        hit = (src // nf == j) & (src % nf == cc % nf)
        e_c = jnp.where(hit & (cc < ROT_DIM), 1.0, 0.0).astype(jnp.bfloat16)
        e_1 = jnp.where(hit & (cc < nf), -1.0, 0.0).astype(jnp.bfloat16)
        e_2 = jnp.where(hit & (cc >= nf) & (cc < ROT_DIM), 1.0, 0.0).astype(jnp.bfloat16)
        rows = pl.ds(j, n_rows, stride=ROPE_PACK)
        for ref, tab3, e, base in ((c_ref, cos3, e_c, 1.0), (s1_ref, sin3, e_1, 0.0), (s2_ref, sin3, e_2, 0.0)):
            t3 = jnp.dot(tab3, e, preferred_element_type=jnp.float32)
            t = (t3[0:n_rows] + t3[n_rows:2 * n_rows]) + t3[2 * n_rows:]
            if base:
                t = t + jnp.where(dst[0:1] % HEAD_DIM < ROT_DIM, 0.0, base)
            ref[0, rows, :] = t


def _rope_tables(positions):
    b, s = positions.shape
    packed = jnp.repeat(positions.reshape(b, s // ROPE_PACK, ROPE_PACK), ROT_DIM // 2, axis=-1)
    spec = pl.BlockSpec((1, s, LANES), lambda i: (i, 0, 0))
    shp = jax.ShapeDtypeStruct((b, s, LANES), jnp.float32)
    return pl.pallas_call(
        _rope_kernel,
        grid=(b,),
        in_specs=[pl.BlockSpec((1, s // ROPE_PACK, LANES), lambda i: (i, 0, 0))],
        out_specs=[spec, spec, spec],
        out_shape=[shp, shp, shp],
        compiler_params=_cparams(1),
        name="rope_tables",
    )(packed)


def _residual_update(o_ref, wout_ref, x, gate, npost):
    cat = jnp.concatenate([o_ref[0, j] for j in range(N_UNITS)], axis=1)
    y = jnp.dot(cat, wout_ref[...], preferred_element_type=jnp.float32)
    var = jnp.mean(y * y, axis=-1, keepdims=True)
    return x + gate * (y * lax.rsqrt(var + NORM_EPS) * npost)


def _inproj_kernel(*refs, rope_slabs, with_forget, with_residual):
    refs = list(refs)
    if with_residual:
        o_ref, wout_ref, gate_ref, npost_ref = refs[:4]
        refs = refs[4:]
    x_ref, shift_ref, scale_ref, npre_ref, w_ref = refs[:5]
    refs = refs[5:]
    wf_ref = refs.pop(0) if with_forget else None
    c_ref, s1_ref, s2_ref = refs[:3]
    refs = refs[3:]
    xout_ref = refs.pop(0) if with_residual else None
    out_ref = refs.pop(0)
    fa_ref = refs.pop(0) if with_forget else None

    x = x_ref[0]
    if with_residual:
        x = _residual_update(o_ref, wout_ref, x, gate_ref[...], npost_ref[...])
        xout_ref[0] = x
    var = jnp.mean(x * x, axis=-1, keepdims=True)
    h = (x * lax.rsqrt(var + NORM_EPS) * npre_ref[...]) * (1.0 + scale_ref[...]) + shift_ref[...]
    hb = h.astype(jnp.bfloat16)
    cos = c_ref[0]
    s1 = s1_ref[0]
    s2 = s2_ref[0]
    for c in range(N_SLABS * LANES // MXU_COLS):
        r = jnp.dot(hb, w_ref[:, c * MXU_COLS:(c + 1) * MXU_COLS],
                    preferred_element_type=jnp.float32)
        for half in range(MXU_COLS // LANES):
            j = c * (MXU_COLS // LANES) + half
            y = r[:, half * LANES:(half + 1) * LANES]
            if j in rope_slabs:
                y = (y * cos + pltpu.roll(y, LANES - ROT_DIM // 2, 1) * s1
                     + pltpu.roll(y, ROT_DIM // 2, 1) * s2)
            out_ref[0, j] = y.astype(jnp.bfloat16)
    if with_forget:
        fa_ref[0] = jnp.dot(hb, wf_ref[...], preferred_element_type=jnp.float32)


def _mod_spec(layer, which):
    return pl.BlockSpec((None, None, None, 1, D_MODEL), lambda bi, i: (layer, which, bi, 0, 0))


def _layer_spec(shape, index):
    return pl.BlockSpec((None,) + tuple(shape[1:]), lambda bi, i: (index,) + (0,) * (len(shape) - 1))


def _in_projection(x, mod, layer, norm_pre, w, w_idx, wf, rope, rope_slabs, residual=None):
    b, s, d = x.shape
    with_forget = wf is not None
    with_residual = residual is not None
    row = lambda bi, i: (bi, i, 0)
    in_specs, args = [], []
    if with_residual:
        o, w_out, out_idx, norm_post, prev = residual
        in_specs += [pl.BlockSpec((1, N_UNITS, TM, LANES), lambda bi, i: (bi, 0, i, 0)),
                     _layer_spec(w_out.shape, out_idx), _mod_spec(prev, 2), _layer_spec(norm_post.shape, prev)]
        args += [o, w_out, mod, norm_post]
    in_specs += [pl.BlockSpec((1, TM, d), row), _mod_spec(layer, 0), _mod_spec(layer, 1),
                 _layer_spec(norm_pre.shape, layer), _layer_spec(w.shape, w_idx)]
    args += [x, mod, mod, norm_pre, w]
    if with_forget:
        in_specs.append(_layer_spec(wf.shape, w_idx))
        args.append(wf)
    in_specs += [pl.BlockSpec((1, TM, LANES), row)] * 3
    args += list(rope)
    out_specs, out_shape = [], []
    if with_residual:
        out_specs.append(pl.BlockSpec((1, TM, d), row))
        out_shape.append(jax.ShapeDtypeStruct((b, s, d), jnp.float32))
    out_specs.append(pl.BlockSpec((1, N_SLABS, TM, LANES), lambda bi, i: (bi, 0, i, 0)))
    out_shape.append(jax.ShapeDtypeStruct((b, N_SLABS, s, LANES), jnp.bfloat16))
    if with_forget:
        out_specs.append(pl.BlockSpec((1, TM, LANES), row))
        out_shape.append(jax.ShapeDtypeStruct((b, s, LANES), jnp.float32))
    res = list(pl.pallas_call(
        functools.partial(_inproj_kernel, rope_slabs=frozenset(rope_slabs),
                          with_forget=with_forget, with_residual=with_residual),
        grid=(b, s // TM),
        in_specs=in_specs,
        out_specs=out_specs,
        out_shape=out_shape,
        compiler_params=_cparams(2),
        name="in_projection",
    )(*args))
    x_new = res.pop(0) if with_residual else x
    proj = res.pop(0)
    fa = res.pop(0) if with_forget else None
    return proj, fa, x_new


F_BLOCK = 256


def _fgate_kernel(fa_ref, b_ref, g_ref):
    s = fa_ref.shape[1]
    nh = N_HEADS_FOX
    z = fa_ref[0] + b_ref[...]
    logf = jnp.minimum(z, 0.0) - jnp.log(1.0 + jnp.exp(-jnp.abs(z)))
    r_i = lax.broadcasted_iota(jnp.int32, (F_BLOCK, F_BLOCK), 0)
    c_i = lax.broadcasted_iota(jnp.int32, (F_BLOCK, F_BLOCK), 1)
    tri = jnp.where(c_i <= r_i, 1.0, 0.0).astype(jnp.bfloat16)
    lane = lax.broadcasted_iota(jnp.int32, (1, LANES), 1)

    def split3(v):
        hi = v.astype(jnp.bfloat16)
        r1 = v - hi.astype(jnp.float32)
        mid = r1.astype(jnp.bfloat16)
        lo = (r1 - mid.astype(jnp.float32)).astype(jnp.bfloat16)
        return hi, mid, lo

    carry = jnp.zeros((1, LANES), jnp.float32)
    for i in range(s // F_BLOCK):
        rows = slice(i * F_BLOCK, (i + 1) * F_BLOCK)
        parts = jnp.concatenate(split3(logf[rows]), axis=1)
        cs3 = jnp.dot(tri, parts, preferred_element_type=jnp.float32)
        f = (cs3[:, :LANES] + cs3[:, LANES:2 * LANES]) + cs3[:, 2 * LANES:] + carry
        carry = f[F_BLOCK - 1:F_BLOCK, :]
        hi, mid, lo = split3(f * LOG2E)
        packed = (jnp.where(lane < nh, hi.astype(jnp.float32), 0.0)
                  + pltpu.roll(jnp.where(lane < nh, mid.astype(jnp.float32), 0.0), nh, 1)
                  + pltpu.roll(jnp.where(lane < nh, lo.astype(jnp.float32), 0.0), 2 * nh, 1))
        ones_hi = jnp.where((lane >= 3 * nh) & (lane < 6 * nh), 1.0, 0.0)
        ones_lo = jnp.where(lane < 3 * nh, 1.0, 0.0)
        g_ref[0, 0, 0, rows, :] = (ones_hi - packed).astype(jnp.bfloat16)
        g_ref[0, 0, 1, rows, :] = (ones_lo + pltpu.roll(packed, 3 * nh, 1)).astype(jnp.bfloat16)
    g_ref[0, 1] = jnp.zeros((2, s, LANES), jnp.bfloat16)


def _forget_prefix(fa, b_forget, idx):
    b, s, _ = fa.shape
    return pl.pallas_call(
        _fgate_kernel,
        grid=(b,),
        in_specs=[pl.BlockSpec((1, s, LANES), lambda i: (i, 0, 0)),
                  pl.BlockSpec((None, 1, LANES), lambda i: (idx, 0, 0))],
        out_specs=pl.BlockSpec((1, 2, 2, s, LANES), lambda i: (i, 0, 0, 0, 0)),
        out_shape=jax.ShapeDtypeStruct((b, 2, 2, s, LANES), jnp.bfloat16),
        compiler_params=_cparams(1),
        name="forget_prefix",
    )(fa, b_forget)


V_ROWS = LANES + 16


KPQ = TQ // TK


def _n_bias_tables(mode, s_len):
    if mode == "even":
        return KPQ
    bounded = max((w // d) * d for (w, d) in DILATED_CONFIGS if (w // d) * d < s_len - 1)
    dist = 0
    while dist * TK - (TK - 1) <= bounded:
        dist += 1
    return dist + KPQ


def _bias_tables(mode, s_len):
    kr = np.arange(TK)[:, None]
    qc = np.arange(TQ)[None, :]
    tabs = []
    n_tab = _n_bias_tables(mode, s_len)
    for idx in range(n_tab):
        delta = (idx - (KPQ - 1)) * TK + qc - kr
        off = idx
        if mode == "even":
            mult = (delta >= 0).astype(np.float64)
        else:
            mult = np.zeros((TK, TQ))
            for (w, d) in DILATED_CONFIGS:
                ok = (delta >= 0) & (delta % d == 0) & (delta <= (w // d) * d)
                if off == n_tab - 1:
                    ok = ok & ((w // d) * d >= s_len - 1)
                mult += ok
        tabs.append(np.where(mult > 0, np.log2(np.maximum(mult, 1.0)), NEG))
    return jnp.asarray(np.stack(tabs), jnp.float32)


def _attn_kernel(*refs, mode, lam_init, par_idx):
    if mode == "even":
        (q_ref, k_ref, v_ref, g_ref, aug_ref, tab_ref, lq1_ref, lk1_ref, lq2_ref, lk2_ref, subln_ref, o_ref,
         vt, sb00, sb01, sb10, sb11, cm_buf, m1, a1, m2, a2) = refs
    else:
        (q_ref, k_ref, v_ref, g_ref, tab_ref, o_ref, vt, sb00, sb01, sb10, sb11, cm_buf, m1, a1, m2, a2) = refs
    s_len = q_ref.shape[2]

    def unit(uu, carry):
        u = pl.program_id(1) * UNITS_PER_STEP + uu
        lane = lax.broadcasted_iota(jnp.int32, (1, LANES), 1)
        lo = lane < HEAD_DIM
        nh = N_HEADS_FOX
        nt = (((1,), (1,)), ((), ()))

        vtt = v_ref[0, uu].astype(jnp.float32).T
        ones_rows = jnp.where(lax.broadcasted_iota(jnp.int32, (V_ROWS - LANES, TK), 0) == 0, 1.0, 0.0)
        for j in range(s_len // TK):
            vt[j, 0:LANES, :] = vtt[:, j * TK:(j + 1) * TK].astype(jnp.bfloat16)
            vt[j, LANES:V_ROWS, :] = ones_rows.astype(jnp.bfloat16)

        if mode == "even":
            is_fox = u < N_UNITS // 2

        state = ((m1, a1), (m2, a2))
        n_tab = tab_ref.shape[0]

        def make_lhs(qi):
            q0 = qi * TQ
            qs = q_ref[0, uu, q0:q0 + TQ, :].astype(jnp.float32) * (HEAD_DIM ** -0.5 * LOG2E)
            lhs = [jnp.where(lo, qs, 0.0).astype(jnp.bfloat16),
                   jnp.where(lo, 0.0, qs).astype(jnp.bfloat16)]
            if mode == "even":
                gq = aug_ref[0, 0, 1, q0:q0 + TQ, :].astype(jnp.float32)
                for i in range(2):
                    h = 2 * u + i
                    sel = (lane % nh == h) & (lane < 6 * nh)
                    qa = jnp.where(sel, gq, 0.0)
                    lhs[i] = jnp.concatenate([lhs[i], qa.astype(jnp.bfloat16)], axis=1)
            return lhs

        half = TK // 2
        assert TQ == TK
        s_bufs = ((sb00, sb01), (sb10, sb11))
        z0 = pl.multiple_of(jnp.minimum(pl.program_id(0), 0), TK)

        def scores_into(slot, lhs, qi, kt):
            k0 = kt * TK

            def keys(r0, n):
                kb = k_ref[0, uu, r0:r0 + n, :]
                if mode == "even":
                    kb = jnp.concatenate([kb, aug_ref[0, 0, 0, r0:r0 + n, :]], axis=1)
                return kb

            dist = qi - kt
            for i in range(2):
                if dist > 0:
                    st = lax.dot_general(keys(k0, TK), lhs[i], nt,
                                         preferred_element_type=jnp.float32)
                    if mode != "even":
                        st = st + tab_ref[min(dist, n_tab - 1)]
                    s_bufs[slot][i][pl.ds(z0, TK), :] = st
                    cm_buf[slot, i] = jnp.max(st, axis=0, keepdims=True)
                else:
                    sa = lax.dot_general(keys(k0, half), lhs[i], nt,
                                         preferred_element_type=jnp.float32) + tab_ref[0, 0:half, :]
                    sb = lax.dot_general(keys(k0 + half, half), lhs[i][half:, :], nt,
                                         preferred_element_type=jnp.float32) + tab_ref[0, half:, half:]
                    s_bufs[slot][i][pl.ds(z0, half), :] = sa
                    s_bufs[slot][i][pl.ds(z0 + half, half), half:] = sb
                    ca = jnp.max(sa, axis=0, keepdims=True)
                    cb = jnp.max(sb, axis=0, keepdims=True)
                    cm_buf[slot, i] = jnp.concatenate(
                        [ca[:, :half], jnp.maximum(ca[:, half:], cb)], axis=1)

        def consume(slot, qi, kt):
            for i in range(2):
                m_r, a_r = state[i]
                m_prev = m_r[...]
                m_next = jnp.maximum(m_prev, cm_buf[slot, i])
                alpha = jnp.exp2(m_prev - m_next)
                if qi > kt:
                    pt = jnp.exp2(s_bufs[slot][i][pl.ds(z0, TK), :] - m_next).astype(jnp.bfloat16)
                    a_r[...] = alpha * a_r[...] + jnp.dot(vt[kt], pt, preferred_element_type=jnp.float32)
                else:
                    pa = jnp.exp2(s_bufs[slot][i][pl.ds(z0, half), :] - m_next).astype(jnp.bfloat16)
                    pb = jnp.exp2(s_bufs[slot][i][pl.ds(z0 + half, half), half:]
                                  - m_next[:, half:]).astype(jnp.bfloat16)
                    pva = jnp.dot(vt[kt, :, 0:half], pa, preferred_element_type=jnp.float32)
                    pvb = jnp.dot(vt[kt, :, half:], pb, preferred_element_type=jnp.float32)
                    a_r[:, 0:half] = alpha[:, :half] * a_r[:, 0:half] + pva[:, :half]
                    a_r[:, half:] = alpha[:, half:] * a_r[:, half:] + (pva[:, half:] + pvb)
                m_r[...] = m_next

        def finish(qi):
            q0 = qi * TQ
            outs = []
            for (m_r, a_r) in state:
                acc = a_r[...]
                inv_l = 1.0 / acc[LANES:LANES + 1, :]
                outs.append((acc[0:LANES, :] * inv_l).T)
            o1, o2 = outs
            g = g_ref[0, uu, q0:q0 + TQ, :].astype(jnp.float32)
            gate = g * jax.nn.sigmoid(g)
            out = jnp.where(lo, o1, o2)
            if mode == "even":
                pr = slice(par_idx, par_idx + 1)
                lam = (jnp.exp(jnp.sum(lq1_ref[pr, :] * lk1_ref[pr, :], axis=1, keepdims=True))
                       - jnp.exp(jnp.sum(lq2_ref[pr, :] * lk2_ref[pr, :], axis=1, keepdims=True)) + lam_init)
                d = o1 - lam * o2
                var = jnp.mean(d * d, axis=-1, keepdims=True)
                dn = (d * lax.rsqrt(var + NORM_EPS) * subln_ref[pr, :]) * (1.0 - lam_init)
                out = jnp.where(is_fox, out, dn)
            o_ref[0, uu, q0:q0 + TQ, :] = (out * gate).astype(jnp.bfloat16)

        pairs = [(qi, kt) for qi in range(s_len // TQ) for kt in range(qi + 1)]
        lhs = make_lhs(0)
        scores_into(0, lhs, 0, 0)
        for n, (qi, kt) in enumerate(pairs):
            if kt == 0:
                for (m_r, a_r) in state:
                    m_r[...] = jnp.full((1, TQ), NEG, jnp.float32)
                    a_r[...] = jnp.zeros((V_ROWS, TQ), jnp.float32)
            if n + 1 < len(pairs):
                nqi, nkt = pairs[n + 1]
                if nkt == 0:
                    lhs = make_lhs(nqi)
                scores_into((n + 1) % 2, lhs, nqi, nkt)
            consume(n % 2, qi, kt)
            if kt == qi:
                finish(qi)
        return carry

    lax.fori_loop(0, UNITS_PER_STEP, unit, 0)


def _attention(proj, mode, aug=None, params=None, par_idx=0, lam_init=0.0):
    b, _, s, _ = proj.shape
    tabs = _bias_tables(mode, s)

    def slab(kind):
        per = N_UNITS // UNITS_PER_STEP
        if mode == "even":
            half_blocks = per // 2
            return pl.BlockSpec((1, UNITS_PER_STEP, s, LANES),
                                lambda bi, p: (bi, kind * half_blocks + p + 3 * half_blocks * (p // half_blocks), 0, 0))
        return pl.BlockSpec((1, UNITS_PER_STEP, s, LANES), lambda bi, p: (bi, kind * per + p, 0, 0))

    in_specs = [slab(0), slab(1), slab(2), slab(3)]
    args = [proj, proj, proj, proj]
    scratch = []
    if mode == "even":
        in_specs.append(pl.BlockSpec((1, 1, 2, s, LANES),
                                     lambda bi, p: (bi, p * UNITS_PER_STEP // (N_UNITS // 2), 0, 0, 0)))
        args.append(aug)
    in_specs.append(pl.BlockSpec(tabs.shape, lambda bi, p: (0, 0, 0)))
    args.append(tabs)
    if mode == "even":
        for arr in params:
            in_specs.append(pl.BlockSpec(arr.shape, lambda bi, p: (0, 0)))
            args.append(arr)
    scratch.append(pltpu.VMEM((s // TK, V_ROWS, TK), jnp.bfloat16))
    scratch += [pltpu.VMEM((TK, TQ), jnp.float32)] * 4
    scratch.append(pltpu.VMEM((2, 2, 1, TQ), jnp.float32))
    scratch += [pltpu.VMEM((1, TQ), jnp.float32), pltpu.VMEM((V_ROWS, TQ), jnp.float32)] * 2
    return pl.pallas_call(
        functools.partial(_attn_kernel, mode=mode, lam_init=lam_init, par_idx=par_idx),
        grid=(b, N_UNITS // UNITS_PER_STEP),
        in_specs=in_specs,
        out_specs=pl.BlockSpec((1, UNITS_PER_STEP, s, LANES), lambda bi, p: (bi, p, 0, 0)),
        out_shape=jax.ShapeDtypeStruct((b, N_UNITS, s, LANES), jnp.bfloat16),
        scratch_shapes=scratch,
        compiler_params=_cparams(2),
        name="attention_" + mode,
    )(*args)


def _outproj_kernel(o_ref, w_ref, x_ref, gate_ref, npost_ref, out_ref):
    out_ref[0] = _residual_update(o_ref, w_ref, x_ref[0], gate_ref[...], npost_ref[...])


def _out_projection(o, w_out, out_idx, x, mod, layer, norm_post):
    b, s, d = x.shape
    return pl.pallas_call(
        _outproj_kernel,
        grid=(b, s // TM),
        in_specs=[
            pl.BlockSpec((1, N_UNITS, TM, LANES), lambda bi, i: (bi, 0, i, 0)),
            _layer_spec(w_out.shape, out_idx),
            pl.BlockSpec((1, TM, d), lambda bi, i: (bi, i, 0)),
            _mod_spec(layer, 2),
            _layer_spec(norm_post.shape, layer),
        ],
        out_specs=pl.BlockSpec((1, TM, d), lambda bi, i: (bi, i, 0)),
        out_shape=jax.ShapeDtypeStruct((b, s, d), jnp.float32),
        compiler_params=_cparams(2),
        name="out_projection",
    )(o, w_out, x, mod, norm_post)


def _even_weight_layout(w_in):
    f0 = 3 * N_HEADS_FOX * HEAD_DIM
    f1 = f0 + N_HEADS_FOX
    main = jnp.concatenate([w_in[..., :f0], w_in[..., f1:]], axis=2).astype(jnp.bfloat16)
    wf = jnp.pad(w_in[..., f0:f1], ((0, 0), (0, 0), (0, LANES - N_HEADS_FOX))).astype(jnp.bfloat16)
    return main, wf


def kernel(x, c, positions, norm_pre, norm_post, ada_w, ada_b, ev_w_in, ev_b_forget,
           ev_lambda_q1, ev_lambda_k1, ev_lambda_q2, ev_lambda_k2, ev_subln, ev_w_out,
           od_w_in, od_w_out):
    depth, d = norm_pre.shape
    mod = _adaln_mod(c, ada_w, ada_b)
    rope = _rope_tables(positions)
    half = N_UNITS // 2
    even_rope = set(range(4 * half, 6 * half))
    odd_rope = set(range(2 * N_UNITS))
    ev_w_main, ev_wf = _even_weight_layout(ev_w_in)
    od_w_main = od_w_in.astype(jnp.bfloat16)
    w_outs = (ev_w_out.astype(jnp.bfloat16), od_w_out.astype(jnp.bfloat16))
    npre = norm_pre.reshape(depth, 1, d)
    npost = norm_post.reshape(depth, 1, d)
    b_forget = jnp.pad(ev_b_forget, ((0, 0), (0, LANES - N_HEADS_FOX))).reshape(-1, 1, LANES)
    lam_params = (ev_lambda_q1, ev_lambda_k1, ev_lambda_q2, ev_lambda_k2, ev_subln)
    pending = None
    for layer in range(depth):
        idx = layer // 2
        if layer % 2 == 0:
            proj, fa, x = _in_projection(x, mod, layer, npre, ev_w_main, idx, ev_wf, rope, even_rope, pending)
            aug = _forget_prefix(fa, b_forget, idx)
            lam_init = 0.8 - 0.6 * math.exp(-0.3 * layer)
            o = _attention(proj, "even", aug=aug, params=lam_params, par_idx=idx, lam_init=lam_init)
        else:
            proj, _, x = _in_projection(x, mod, layer, npre, od_w_main, idx, None, rope, odd_rope, pending)
            o = _attention(proj, "odd")
        pending = (o, w_outs[layer % 2], idx, npost, layer)
    o, w_out, idx, _, layer = pending
    return _out_projection(o, w_out, idx, x, mod, layer, npost)
```

```python
import functools
import math

import numpy as np
import jax
import jax.numpy as jnp
from jax import lax
from jax.experimental import pallas as pl
from jax.experimental.pallas import tpu as pltpu

D_MODEL = 1024
HEAD_DIM = 64
N_HEADS_FOX = 8
ROT_DIM = HEAD_DIM // 4
ROPE_THETA = 500000.0
NORM_EPS = 1e-6
DILATED_CONFIGS = ((128, 1), (512, 4), (2048, 16))

LANES = 128
MXU_COLS = 256
N_SLABS = 32
N_UNITS = 8
N_KINDS = 4
UNITS_PER_STEP = 1
TQ = 512
TK = 512
TM = 512
NEG = -1e30
LOG2E = math.log2(math.e)
VMEM_LIMIT = 48 * 1024 * 1024


def _cparams(n_axes):
    return pltpu.CompilerParams(
        dimension_semantics=("arbitrary",) * n_axes, vmem_limit_bytes=VMEM_LIMIT)


def _mod_kernel(c_ref, w_ref, b_ref, o_ref):
    c = c_ref[...]
    cond = c * jax.nn.sigmoid(c)
    for j in range(3):
        o_ref[0, j] = (jnp.dot(cond, w_ref[0, :, j * D_MODEL:(j + 1) * D_MODEL],
                               preferred_element_type=jnp.float32) + b_ref[0, j])


def _adaln_mod(c, ada_w, ada_b):
    depth = ada_w.shape[0]
    b = c.shape[0]
    out = pl.pallas_call(
        _mod_kernel,
        grid=(depth,),
        in_specs=[
            pl.BlockSpec((b, D_MODEL), lambda l: (0, 0)),
            pl.BlockSpec((1, D_MODEL, 3 * D_MODEL), lambda l: (l, 0, 0)),
            pl.BlockSpec((1, 3, 1, D_MODEL), lambda l: (l, 0, 0, 0)),
        ],
        out_specs=pl.BlockSpec((1, 3, b, D_MODEL), lambda l: (l, 0, 0, 0)),
        out_shape=jax.ShapeDtypeStruct((depth, 3, b, D_MODEL), jnp.float32),
        compiler_params=_cparams(1),
        name="adaln_mod",
    )(c, ada_w, ada_b.reshape(depth, 3, 1, D_MODEL))
    return out.reshape(depth, 3, b, 1, D_MODEL)


ROPE_PACK = LANES // (ROT_DIM // 2)


def _rope_kernel(pos_ref, c_ref, s1_ref, s2_ref):
    nf = ROT_DIM // 2
    pos = pos_ref[0].astype(jnp.float32)
    lane = lax.broadcasted_iota(jnp.int32, (1, LANES), 1)
    fidx = (lane % nf).astype(jnp.float32)
    theta = jnp.full((1, LANES), ROPE_THETA, jnp.float32)
    inv_freq = jnp.exp(-(fidx * (2.0 / ROT_DIM)) * jnp.log(theta))
    ang = pos * inv_freq
    n_rows = pos.shape[0]

    def split3(v):
        hi = v.astype(jnp.bfloat16)
        r1 = v - hi.astype(jnp.float32)
        mid = r1.astype(jnp.bfloat16)
        lo = (r1 - mid.astype(jnp.float32)).astype(jnp.bfloat16)
        return jnp.concatenate([hi, mid, lo], axis=0)

    cos3 = split3(jnp.cos(ang))
    sin3 = split3(jnp.sin(ang))
    src = lax.broadcasted_iota(jnp.int32, (LANES, LANES), 0)
    dst = lax.broadcasted_iota(jnp.int32, (LANES, LANES), 1)
    cc = dst % HEAD_DIM
    for j in range(ROPE_PACK):
        hit = (src // nf == j) & (src % nf == cc % nf)
        e_c = jnp.where(hit & (cc < ROT_DIM), 1.0, 0.0).astype(jnp.bfloat16)
        e_1 = jnp.where(hit & (cc < nf), -1.0, 0.0).astype(jnp.bfloat16)
        e_2 = jnp.where(hit & (cc >= nf) & (cc < ROT_DIM), 1.0, 0.0).astype(jnp.bfloat16)
        rows = pl.ds(j, n_rows, stride=ROPE_PACK)
        for ref, tab3, e, base in ((c_ref, cos3, e_c, 1.0), (s1_ref, sin3, e_1, 0.0), (s2_ref, sin3, e_2, 0.0)):
            t3 = jnp.dot(tab3, e, preferred_element_type=jnp.float32)
            t = (t3[0:n_rows] + t3[n_rows:2 * n_rows]) + t3[2 * n_rows:]
            if base:
                t = t + jnp.where(dst[0:1] % HEAD_DIM < ROT_DIM, 0.0, base)
            ref[0, rows, :] = t


def _rope_tables(positions):
    b, s = positions.shape
    packed = jnp.repeat(positions.reshape(b, s // ROPE_PACK, ROPE_PACK), ROT_DIM // 2, axis=-1)
    spec = pl.BlockSpec((1, s, LANES), lambda i: (i, 0, 0))
    shp = jax.ShapeDtypeStruct((b, s, LANES), jnp.float32)
    return pl.pallas_call(
        _rope_kernel,
        grid=(b,),
        in_specs=[pl.BlockSpec((1, s // ROPE_PACK, LANES), lambda i: (i, 0, 0))],
        out_specs=[spec, spec, spec],
        out_shape=[shp, shp, shp],
        compiler_params=_cparams(1),
        name="rope_tables",
    )(packed)


def _residual_update(o_ref, wout_ref, x, gate, npost):
    cat = jnp.concatenate([o_ref[0, j] for j in range(N_UNITS)], axis=1)
    y = jnp.dot(cat, wout_ref[...], preferred_element_type=jnp.float32)
    var = jnp.mean(y * y, axis=-1, keepdims=True)
    return x + gate * (y * lax.rsqrt(var + NORM_EPS) * npost)


def _inproj_kernel(*refs, rope_slabs, slab_dst, with_forget, with_residual):
    refs = list(refs)
    if with_residual:
        o_ref, wout_ref, gate_ref, npost_ref = refs[:4]
        refs = refs[4:]
    x_ref, shift_ref, scale_ref, npre_ref, w_ref = refs[:5]
    refs = refs[5:]
    wf_ref = refs.pop(0) if with_forget else None
    c_ref, s1_ref, s2_ref = refs[:3]
    refs = refs[3:]
    xout_ref = refs.pop(0) if with_residual else None
    out_ref = refs.pop(0)
    fa_ref = refs.pop(0) if with_forget else None

    x = x_ref[0]
    if with_residual:
        x = _residual_update(o_ref, wout_ref, x, gate_ref[...], npost_ref[...])
        xout_ref[0] = x
    var = jnp.mean(x * x, axis=-1, keepdims=True)
    h = (x * lax.rsqrt(var + NORM_EPS) * npre_ref[...]) * (1.0 + scale_ref[...]) + shift_ref[...]
    hb = h.astype(jnp.bfloat16)
    cos = c_ref[0]
    s1 = s1_ref[0]
    s2 = s2_ref[0]
    for c in range(N_SLABS * LANES // MXU_COLS):
        r = jnp.dot(hb, w_ref[:, c * MXU_COLS:(c + 1) * MXU_COLS],
                    preferred_element_type=jnp.float32)
        for half in range(MXU_COLS // LANES):
            j = c * (MXU_COLS // LANES) + half
            y = r[:, half * LANES:(half + 1) * LANES]
            if j in rope_slabs:
                y = (y * cos + pltpu.roll(y, LANES - ROT_DIM // 2, 1) * s1
                     + pltpu.roll(y, ROT_DIM // 2, 1) * s2)
            out_ref[(0,) + slab_dst[j]] = y.astype(jnp.bfloat16)
    if with_forget:
        fa_ref[0] = jnp.dot(hb, wf_ref[...], preferred_element_type=jnp.float32)


def _mod_spec(layer, which):
    return pl.BlockSpec((None, None, None, 1, D_MODEL), lambda bi, i: (layer, which, bi, 0, 0))


def _layer_spec(shape, index):
    return pl.BlockSpec((None,) + tuple(shape[1:]), lambda bi, i: (index,) + (0,) * (len(shape) - 1))


def _in_projection(x, mod, layer, norm_pre, w, w_idx, wf, rope, rope_slabs, slab_dst, residual=None):
    b, s, d = x.shape
    with_forget = wf is not None
    with_residual = residual is not None
    row = lambda bi, i: (bi, i, 0)
    in_specs, args = [], []
    if with_residual:
        o, w_out, out_idx, norm_post, prev = residual
        in_specs += [pl.BlockSpec((1, N_UNITS, TM, LANES), lambda bi, i: (bi, 0, i, 0)),
                     _layer_spec(w_out.shape, out_idx), _mod_spec(prev, 2), _layer_spec(norm_post.shape, prev)]
        args += [o, w_out, mod, norm_post]
    in_specs += [pl.BlockSpec((1, TM, d), row), _mod_spec(layer, 0), _mod_spec(layer, 1),
                 _layer_spec(norm_pre.shape, layer), _layer_spec(w.shape, w_idx)]
    args += [x, mod, mod, norm_pre, w]
    if with_forget:
        in_specs.append(_layer_spec(wf.shape, w_idx))
        args.append(wf)
    in_specs += [pl.BlockSpec((1, TM, LANES), row)] * 3
    args += list(rope)
    out_specs, out_shape = [], []
    if with_residual:
        out_specs.append(pl.BlockSpec((1, TM, d), row))
        out_shape.append(jax.ShapeDtypeStruct((b, s, d), jnp.float32))
    out_specs.append(pl.BlockSpec((1, N_UNITS, N_KINDS, TM, LANES), lambda bi, i: (bi, 0, 0, i, 0)))
    out_shape.append(jax.ShapeDtypeStruct((b, N_UNITS, N_KINDS, s, LANES), jnp.bfloat16))
    if with_forget:
        out_specs.append(pl.BlockSpec((1, TM, LANES), row))
        out_shape.append(jax.ShapeDtypeStruct((b, s, LANES), jnp.float32))
    res = list(pl.pallas_call(
        functools.partial(_inproj_kernel, rope_slabs=frozenset(rope_slabs), slab_dst=tuple(slab_dst),
                          with_forget=with_forget, with_residual=with_residual),
        grid=(b, s // TM),
        in_specs=in_specs,
        out_specs=out_specs,
        out_shape=out_shape,
        compiler_params=_cparams(2),
        name="in_projection",
    )(*args))
    x_new = res.pop(0) if with_residual else x
    proj = res.pop(0)
    fa = res.pop(0) if with_forget else None
    return proj, fa, x_new


F_BLOCK = 256


def _fgate_kernel(fa_ref, b_ref, g_ref):
    s = fa_ref.shape[1]
    nh = N_HEADS_FOX
    z = fa_ref[0] + b_ref[...]
    logf = jnp.minimum(z, 0.0) - jnp.log(1.0 + jnp.exp(-jnp.abs(z)))
    r_i = lax.broadcasted_iota(jnp.int32, (F_BLOCK, F_BLOCK), 0)
    c_i = lax.broadcasted_iota(jnp.int32, (F_BLOCK, F_BLOCK), 1)
    tri = jnp.where(c_i <= r_i, 1.0, 0.0).astype(jnp.bfloat16)
    lane = lax.broadcasted_iota(jnp.int32, (1, LANES), 1)

    def split3(v):
        hi = v.astype(jnp.bfloat16)
        r1 = v - hi.astype(jnp.float32)
        mid = r1.astype(jnp.bfloat16)
        lo = (r1 - mid.astype(jnp.float32)).astype(jnp.bfloat16)
        return hi, mid, lo

    carry = jnp.zeros((1, LANES), jnp.float32)
    for i in range(s // F_BLOCK):
        rows = slice(i * F_BLOCK, (i + 1) * F_BLOCK)
        parts = jnp.concatenate(split3(logf[rows]), axis=1)
        cs3 = jnp.dot(tri, parts, preferred_element_type=jnp.float32)
        f = (cs3[:, :LANES] + cs3[:, LANES:2 * LANES]) + cs3[:, 2 * LANES:] + carry
        carry = f[F_BLOCK - 1:F_BLOCK, :]
        hi, mid, lo = split3(f * LOG2E)
        packed = (jnp.where(lane < nh, hi.astype(jnp.float32), 0.0)
                  + pltpu.roll(jnp.where(lane < nh, mid.astype(jnp.float32), 0.0), nh, 1)
                  + pltpu.roll(jnp.where(lane < nh, lo.astype(jnp.float32), 0.0), 2 * nh, 1))
        ones_hi = jnp.where((lane >= 3 * nh) & (lane < 6 * nh), 1.0, 0.0)
        ones_lo = jnp.where(lane < 3 * nh, 1.0, 0.0)
        g_ref[0, 0, 0, rows, :] = (ones_hi - packed).astype(jnp.bfloat16)
        g_ref[0, 0, 1, rows, :] = (ones_lo + pltpu.roll(packed, 3 * nh, 1)).astype(jnp.bfloat16)
    g_ref[0, 1] = jnp.zeros((2, s, LANES), jnp.bfloat16)


def _forget_prefix(fa, b_forget, idx):
    b, s, _ = fa.shape
    return pl.pallas_call(
        _fgate_kernel,
        grid=(b,),
        in_specs=[pl.BlockSpec((1, s, LANES), lambda i: (i, 0, 0)),
                  pl.BlockSpec((None, 1, LANES), lambda i: (idx, 0, 0))],
        out_specs=pl.BlockSpec((1, 2, 2, s, LANES), lambda i: (i, 0, 0, 0, 0)),
        out_shape=jax.ShapeDtypeStruct((b, 2, 2, s, LANES), jnp.bfloat16),
        compiler_params=_cparams(1),
        name="forget_prefix",
    )(fa, b_forget)


V_ROWS = LANES + 16


KPQ = TQ // TK


def _n_bias_tables(mode, s_len):
    if mode == "even":
        return KPQ
    bounded = max((w // d) * d for (w, d) in DILATED_CONFIGS if (w // d) * d < s_len - 1)
    dist = 0
    while dist * TK - (TK - 1) <= bounded:
        dist += 1
    return dist + KPQ


def _bias_tables(mode, s_len):
    kr = np.arange(TK)[:, None]
    qc = np.arange(TQ)[None, :]
    tabs = []
    n_tab = _n_bias_tables(mode, s_len)
    for idx in range(n_tab):
        delta = (idx - (KPQ - 1)) * TK + qc - kr
        off = idx
        if mode == "even":
            mult = (delta >= 0).astype(np.float64)
        else:
            mult = np.zeros((TK, TQ))
            for (w, d) in DILATED_CONFIGS:
                ok = (delta >= 0) & (delta % d == 0) & (delta <= (w // d) * d)
                if off == n_tab - 1:
                    ok = ok & ((w // d) * d >= s_len - 1)
                mult += ok
        tabs.append(np.where(mult > 0, np.log2(np.maximum(mult, 1.0)), NEG))
    return jnp.asarray(np.stack(tabs), jnp.float32)


def _attn_kernel(*refs, mode, lam_init, par_idx):
    if mode == "even":
        (p_ref, aug_ref, tab_ref, lq1_ref, lk1_ref, lq2_ref, lk2_ref, subln_ref, o_ref,
         vt, sb00, sb01, sb10, sb11, cm_buf, m1, a1, m2, a2) = refs
    else:
        (p_ref, tab_ref, o_ref, vt, sb00, sb01, sb10, sb11, cm_buf, m1, a1, m2, a2) = refs
    s_len = p_ref.shape[3]

    def unit(uu, carry):
        u = pl.program_id(1) * UNITS_PER_STEP + uu
        lane = lax.broadcasted_iota(jnp.int32, (1, LANES), 1)
        lo = lane < HEAD_DIM
        nh = N_HEADS_FOX
        nt = (((1,), (1,)), ((), ()))

        vtt = p_ref[0, uu, 2].astype(jnp.float32).T
        ones_rows = jnp.where(lax.broadcasted_iota(jnp.int32, (V_ROWS - LANES, TK), 0) == 0, 1.0, 0.0)
        for j in range(s_len // TK):
            vt[j, 0:LANES, :] = vtt[:, j * TK:(j + 1) * TK].astype(jnp.bfloat16)
            vt[j, LANES:V_ROWS, :] = ones_rows.astype(jnp.bfloat16)

        if mode == "even":
            is_fox = u < N_UNITS // 2

        state = ((m1, a1), (m2, a2))
        n_tab = tab_ref.shape[0]

        def make_lhs(qi):
            q0 = qi * TQ
            qs = p_ref[0, uu, 0, q0:q0 + TQ, :].astype(jnp.float32) * (HEAD_DIM ** -0.5 * LOG2E)
            lhs = [jnp.where(lo, qs, 0.0).astype(jnp.bfloat16),
                   jnp.where(lo, 0.0, qs).astype(jnp.bfloat16)]
            if mode == "even":
                gq = aug_ref[0, 0, 1, q0:q0 + TQ, :].astype(jnp.float32)
                for i in range(2):
                    h = 2 * u + i
                    sel = (lane % nh == h) & (lane < 6 * nh)
                    qa = jnp.where(sel, gq, 0.0)
                    lhs[i] = jnp.concatenate([lhs[i], qa.astype(jnp.bfloat16)], axis=1)
            return lhs

        half = TK // 2
        assert TQ == TK
        s_bufs = ((sb00, sb01), (sb10, sb11))
        z0 = pl.multiple_of(jnp.minimum(pl.program_id(0), 0), TK)

        def scores_into(slot, lhs, qi, kt):
            k0 = kt * TK

            def keys(r0, n):
                kb = p_ref[0, uu, 1, r0:r0 + n, :]
                if mode == "even":
                    kb = jnp.concatenate([kb, aug_ref[0, 0, 0, r0:r0 + n, :]], axis=1)
                return kb

            dist = qi - kt
            for i in range(2):
                if dist > 0:
                    st = lax.dot_general(keys(k0, TK), lhs[i], nt,
                                         preferred_element_type=jnp.float32)
                    if mode != "even":
                        st = st + tab_ref[min(dist, n_tab - 1)]
                    s_bufs[slot][i][pl.ds(z0, TK), :] = st
                    cm_buf[slot, i] = jnp.max(st, axis=0, keepdims=True)
                else:
                    sa = lax.dot_general(keys(k0, half), lhs[i], nt,
                                         preferred_element_type=jnp.float32) + tab_ref[0, 0:half, :]
                    sb = lax.dot_general(keys(k0 + half, half), lhs[i][half:, :], nt,
                                         preferred_element_type=jnp.float32) + tab_ref[0, half:, half:]
                    s_bufs[slot][i][pl.ds(z0, half), :] = sa
                    s_bufs[slot][i][pl.ds(z0 + half, half), half:] = sb
                    ca = jnp.max(sa, axis=0, keepdims=True)
                    cb = jnp.max(sb, axis=0, keepdims=True)
                    cm_buf[slot, i] = jnp.concatenate(
                        [ca[:, :half], jnp.maximum(ca[:, half:], cb)], axis=1)

        def consume(slot, qi, kt):
            for i in range(2):
                m_r, a_r = state[i]
                m_prev = m_r[...]
                m_next = jnp.maximum(m_prev, cm_buf[slot, i])
                alpha = jnp.exp2(m_prev - m_next)
                if qi > kt:
                    pt = jnp.exp2(s_bufs[slot][i][pl.ds(z0, TK), :] - m_next).astype(jnp.bfloat16)
                    a_r[...] = alpha * a_r[...] + jnp.dot(vt[kt], pt, preferred_element_type=jnp.float32)
                else:
                    pa = jnp.exp2(s_bufs[slot][i][pl.ds(z0, half), :] - m_next).astype(jnp.bfloat16)
                    pb = jnp.exp2(s_bufs[slot][i][pl.ds(z0 + half, half), half:]
                                  - m_next[:, half:]).astype(jnp.bfloat16)
                    pva = jnp.dot(vt[kt, :, 0:half], pa, preferred_element_type=jnp.float32)
                    pvb = jnp.dot(vt[kt, :, half:], pb, preferred_element_type=jnp.float32)
                    a_r[:, 0:half] = alpha[:, :half] * a_r[:, 0:half] + pva[:, :half]
                    a_r[:, half:] = alpha[:, half:] * a_r[:, half:] + (pva[:, half:] + pvb)
                m_r[...] = m_next

        def finish(qi):
            q0 = qi * TQ
            outs = []
            for (m_r, a_r) in state:
                acc = a_r[...]
                inv_l = 1.0 / acc[LANES:LANES + 1, :]
                outs.append((acc[0:LANES, :] * inv_l).T)
            o1, o2 = outs
            g = p_ref[0, uu, 3, q0:q0 + TQ, :].astype(jnp.float32)
            gate = g * jax.nn.sigmoid(g)
            out = jnp.where(lo, o1, o2)
            if mode == "even":
                pr = slice(par_idx, par_idx + 1)
                lam = (jnp.exp(jnp.sum(lq1_ref[pr, :] * lk1_ref[pr, :], axis=1, keepdims=True))
                       - jnp.exp(jnp.sum(lq2_ref[pr, :] * lk2_ref[pr, :], axis=1, keepdims=True)) + lam_init)
                d = o1 - lam * o2
                var = jnp.mean(d * d, axis=-1, keepdims=True)
                dn = (d * lax.rsqrt(var + NORM_EPS) * subln_ref[pr, :]) * (1.0 - lam_init)
                out = jnp.where(is_fox, out, dn)
            o_ref[0, uu, q0:q0 + TQ, :] = (out * gate).astype(jnp.bfloat16)

        pairs = [(qi, kt) for qi in range(s_len // TQ) for kt in range(qi + 1)]
        lhs = make_lhs(0)
        scores_into(0, lhs, 0, 0)
        for n, (qi, kt) in enumerate(pairs):
            if kt == 0:
                for (m_r, a_r) in state:
                    m_r[...] = jnp.full((1, TQ), NEG, jnp.float32)
                    a_r[...] = jnp.zeros((V_ROWS, TQ), jnp.float32)
            if n + 1 < len(pairs):
                nqi, nkt = pairs[n + 1]
                if nkt == 0:
                    lhs = make_lhs(nqi)
                scores_into((n + 1) % 2, lhs, nqi, nkt)
            consume(n % 2, qi, kt)
            if kt == qi:
                finish(qi)
        return carry

    lax.fori_loop(0, UNITS_PER_STEP, unit, 0)


def _attention(proj, mode, aug=None, params=None, par_idx=0, lam_init=0.0):
    b, _, _, s, _ = proj.shape
    tabs = _bias_tables(mode, s)
    in_specs = [pl.BlockSpec((1, UNITS_PER_STEP, N_KINDS, s, LANES), lambda bi, p: (bi, p, 0, 0, 0))]
    args = [proj]
    scratch = []
    if mode == "even":
        in_specs.append(pl.BlockSpec((1, 1, 2, s, LANES),
                                     lambda bi, p: (bi, p * UNITS_PER_STEP // (N_UNITS // 2), 0, 0, 0)))
        args.append(aug)
    in_specs.append(pl.BlockSpec(tabs.shape, lambda bi, p: (0, 0, 0)))
    args.append(tabs)
    if mode == "even":
        for arr in params:
            in_specs.append(pl.BlockSpec(arr.shape, lambda bi, p: (0, 0)))
            args.append(arr)
    scratch.append(pltpu.VMEM((s // TK, V_ROWS, TK), jnp.bfloat16))
    scratch += [pltpu.VMEM((TK, TQ), jnp.float32)] * 4
    scratch.append(pltpu.VMEM((2, 2, 1, TQ), jnp.float32))
    scratch += [pltpu.VMEM((1, TQ), jnp.float32), pltpu.VMEM((V_ROWS, TQ), jnp.float32)] * 2
    return pl.pallas_call(
        functools.partial(_attn_kernel, mode=mode, lam_init=lam_init, par_idx=par_idx),
        grid=(b, N_UNITS // UNITS_PER_STEP),
        in_specs=in_specs,
        out_specs=pl.BlockSpec((1, UNITS_PER_STEP, s, LANES), lambda bi, p: (bi, p, 0, 0)),
        out_shape=jax.ShapeDtypeStruct((b, N_UNITS, s, LANES), jnp.bfloat16),
        scratch_shapes=scratch,
        compiler_params=_cparams(2),
        name="attention_" + mode,
    )(*args)


def _outproj_kernel(o_ref, w_ref, x_ref, gate_ref, npost_ref, out_ref):
    out_ref[0] = _residual_update(o_ref, w_ref, x_ref[0], gate_ref[...], npost_ref[...])


def _out_projection(o, w_out, out_idx, x, mod, layer, norm_post):
    b, s, d = x.shape
    return pl.pallas_call(
        _outproj_kernel,
        grid=(b, s // TM),
        in_specs=[
            pl.BlockSpec((1, N_UNITS, TM, LANES), lambda bi, i: (bi, 0, i, 0)),
            _layer_spec(w_out.shape, out_idx),
            pl.BlockSpec((1, TM, d), lambda bi, i: (bi, i, 0)),
            _mod_spec(layer, 2),
            _layer_spec(norm_post.shape, layer),
        ],
        out_specs=pl.BlockSpec((1, TM, d), lambda bi, i: (bi, i, 0)),
        out_shape=jax.ShapeDtypeStruct((b, s, d), jnp.float32),
        compiler_params=_cparams(2),
        name="out_projection",
    )(o, w_out, x, mod, norm_post)


def _even_weight_layout(w_in):
    f0 = 3 * N_HEADS_FOX * HEAD_DIM
    f1 = f0 + N_HEADS_FOX
    main = jnp.concatenate([w_in[..., :f0], w_in[..., f1:]], axis=2).astype(jnp.bfloat16)
    wf = jnp.pad(w_in[..., f0:f1], ((0, 0), (0, 0), (0, LANES - N_HEADS_FOX))).astype(jnp.bfloat16)
    return main, wf


def kernel(x, c, positions, norm_pre, norm_post, ada_w, ada_b, ev_w_in, ev_b_forget,
           ev_lambda_q1, ev_lambda_k1, ev_lambda_q2, ev_lambda_k2, ev_subln, ev_w_out,
           od_w_in, od_w_out):
    depth, d = norm_pre.shape
    mod = _adaln_mod(c, ada_w, ada_b)
    rope = _rope_tables(positions)
    half = N_UNITS // 2
    even_dst = ([(j % half, j // half) for j in range(N_KINDS * half)]
                + [(half + j % half, j // half) for j in range(N_KINDS * half)])
    odd_dst = [(j % N_UNITS, j // N_UNITS) for j in range(N_SLABS)]
    even_rope = set(range(4 * half, 6 * half))
    odd_rope = set(range(2 * N_UNITS))
    ev_w_main, ev_wf = _even_weight_layout(ev_w_in)
    od_w_main = od_w_in.astype(jnp.bfloat16)
    w_outs = (ev_w_out.astype(jnp.bfloat16), od_w_out.astype(jnp.bfloat16))
    npre = norm_pre.reshape(depth, 1, d)
    npost = norm_post.reshape(depth, 1, d)
    b_forget = jnp.pad(ev_b_forget, ((0, 0), (0, LANES - N_HEADS_FOX))).reshape(-1, 1, LANES)
    lam_params = (ev_lambda_q1, ev_lambda_k1, ev_lambda_q2, ev_lambda_k2, ev_subln)
    pending = None
    for layer in range(depth):
        idx = layer // 2
        if layer % 2 == 0:
            proj, fa, x = _in_projection(x, mod, layer, npre, ev_w_main, idx, ev_wf, rope, even_rope, even_dst,
                                         pending)
            aug = _forget_prefix(fa, b_forget, idx)
            lam_init = 0.8 - 0.6 * math.exp(-0.3 * layer)
            o = _attention(proj, "even", aug=aug, params=lam_params, par_idx=idx, lam_init=lam_init)
        else:
            proj, _, x = _in_projection(x, mod, layer, npre, od_w_main, idx, None, rope, odd_rope, odd_dst,
                                        pending)
            o = _attention(proj, "odd")
        pending = (o, w_outs[layer % 2], idx, npost, layer)
    o, w_out, idx, _, layer = pending
    return _out_projection(o, w_out, idx, x, mod, layer, npost)
```

```python
import functools
import math

import jax
import jax.numpy as jnp
from jax import lax
from jax.experimental import pallas as pl
from jax.experimental.pallas import tpu as pltpu

D_MODEL = 1024
HEAD_DIM = 64
N_HEADS_FOX = 8
ROT_DIM = HEAD_DIM // 4
ROPE_THETA = 500000.0
NORM_EPS = 1e-6
DILATED_CONFIGS = ((128, 1), (512, 4), (2048, 16))

LANES = 128
MXU_COLS = 256
N_SLABS = 32
N_UNITS = 8
N_KINDS = 4
UNITS_PER_STEP = 1
TQ = 512
TK = 512
TM = 512
NEG = -1e30
LOG2E = math.log2(math.e)
VMEM_LIMIT = 48 * 1024 * 1024


def _cparams(n_axes):
    return pltpu.CompilerParams(
        dimension_semantics=("arbitrary",) * n_axes, vmem_limit_bytes=VMEM_LIMIT)


def _mod_kernel(c_ref, w_ref, b_ref, o_ref):
    c = c_ref[...]
    cond = c * jax.nn.sigmoid(c)
    for j in range(3):
        o_ref[0, j] = (jnp.dot(cond, w_ref[0, :, j * D_MODEL:(j + 1) * D_MODEL],
                               preferred_element_type=jnp.float32) + b_ref[0, j])


def _adaln_mod(c, ada_w, ada_b):
    depth = ada_w.shape[0]
    b = c.shape[0]
    out = pl.pallas_call(
        _mod_kernel,
        grid=(depth,),
        in_specs=[
            pl.BlockSpec((b, D_MODEL), lambda l: (0, 0)),
            pl.BlockSpec((1, D_MODEL, 3 * D_MODEL), lambda l: (l, 0, 0)),
            pl.BlockSpec((1, 3, 1, D_MODEL), lambda l: (l, 0, 0, 0)),
        ],
        out_specs=pl.BlockSpec((1, 3, b, D_MODEL), lambda l: (l, 0, 0, 0)),
        out_shape=jax.ShapeDtypeStruct((depth, 3, b, D_MODEL), jnp.float32),
        compiler_params=_cparams(1),
        name="adaln_mod",
    )(c, ada_w, ada_b.reshape(depth, 3, 1, D_MODEL))
    return out.reshape(depth, 3, b, 1, D_MODEL)


ROPE_PACK = LANES // (ROT_DIM // 2)


def _rope_kernel(pos_ref, c_ref, s1_ref, s2_ref):
    nf = ROT_DIM // 2
    pos = pos_ref[0].astype(jnp.float32)
    lane = lax.broadcasted_iota(jnp.int32, (1, LANES), 1)
    fidx = (lane % nf).astype(jnp.float32)
    theta = jnp.full((1, LANES), ROPE_THETA, jnp.float32)
    inv_freq = jnp.exp(-(fidx * (2.0 / ROT_DIM)) * jnp.log(theta))
    ang = pos * inv_freq
    n_rows = pos.shape[0]

    def split3(v):
        hi = v.astype(jnp.bfloat16)
        r1 = v - hi.astype(jnp.float32)
        mid = r1.astype(jnp.bfloat16)
        lo = (r1 - mid.astype(jnp.float32)).astype(jnp.bfloat16)
        return jnp.concatenate([hi, mid, lo], axis=0)

    cos3 = split3(jnp.cos(ang))
    sin3 = split3(jnp.sin(ang))
    src = lax.broadcasted_iota(jnp.int32, (LANES, LANES), 0)
    dst = lax.broadcasted_iota(jnp.int32, (LANES, LANES), 1)
    cc = dst % HEAD_DIM
    for j in range(ROPE_PACK):
        hit = (src // nf == j) & (src % nf == cc % nf)
        e_c = jnp.where(hit & (cc < ROT_DIM), 1.0, 0.0).astype(jnp.bfloat16)
        e_1 = jnp.where(hit & (cc < nf), -1.0, 0.0).astype(jnp.bfloat16)
        e_2 = jnp.where(hit & (cc >= nf) & (cc < ROT_DIM), 1.0, 0.0).astype(jnp.bfloat16)
        rows = pl.ds(j, n_rows, stride=ROPE_PACK)
        for ref, tab3, e, base in ((c_ref, cos3, e_c, 1.0), (s1_ref, sin3, e_1, 0.0), (s2_ref, sin3, e_2, 0.0)):
            t3 = jnp.dot(tab3, e, preferred_element_type=jnp.float32)
            t = (t3[0:n_rows] + t3[n_rows:2 * n_rows]) + t3[2 * n_rows:]
            if base:
                t = t + jnp.where(dst[0:1] % HEAD_DIM < ROT_DIM, 0.0, base)
            ref[0, rows, :] = t


def _rope_tables(positions):
    b, s = positions.shape
    packed = jnp.repeat(positions.reshape(b, s // ROPE_PACK, ROPE_PACK), ROT_DIM // 2, axis=-1)
    spec = pl.BlockSpec((1, s, LANES), lambda i: (i, 0, 0))
    shp = jax.ShapeDtypeStruct((b, s, LANES), jnp.float32)
    return pl.pallas_call(
        _rope_kernel,
        grid=(b,),
        in_specs=[pl.BlockSpec((1, s // ROPE_PACK, LANES), lambda i: (i, 0, 0))],
        out_specs=[spec, spec, spec],
        out_shape=[shp, shp, shp],
        compiler_params=_cparams(1),
        name="rope_tables",
    )(packed)


def _residual_update(o_ref, wout_ref, x, gate, npost):
    cat = jnp.concatenate([o_ref[0, j] for j in range(N_UNITS)], axis=1)
    y = jnp.dot(cat, wout_ref[...], preferred_element_type=jnp.float32)
    var = jnp.mean(y * y, axis=-1, keepdims=True)
    return x + gate * (y * lax.rsqrt(var + NORM_EPS) * npost)


def _inproj_kernel(*refs, rope_slabs, slab_dst, n_w, with_forget, with_residual):
    refs = list(refs)
    if with_residual:
        o_ref, wout_ref, gate_ref, npost_ref = refs[:4]
        refs = refs[4:]
    x_ref, shift_ref, scale_ref, npre_ref = refs[:4]
    w_refs, refs = refs[4:4 + n_w], refs[4 + n_w:]
    wf_ref = refs.pop(0) if with_forget else None
    c_ref, s1_ref, s2_ref = refs[:3]
    refs = refs[3:]
    xout_ref = refs.pop(0) if with_residual else None
    out_ref = refs.pop(0)
    fa_ref = refs.pop(0) if with_forget else None

    x = x_ref[0]
    if with_residual:
        x = _residual_update(o_ref, wout_ref, x, gate_ref[...], npost_ref[...])
        xout_ref[0] = x
    var = jnp.mean(x * x, axis=-1, keepdims=True)
    h = (x * lax.rsqrt(var + NORM_EPS) * npre_ref[...]) * (1.0 + scale_ref[...]) + shift_ref[...]
    hb = h.astype(jnp.bfloat16)
    cos = c_ref[0]
    s1 = s1_ref[0]
    s2 = s2_ref[0]
    chunks = [(w_ref, k) for w_ref in w_refs for k in range(w_ref.shape[1] // MXU_COLS)]
    assert len(chunks) == N_SLABS * LANES // MXU_COLS
    for c, (w_ref, k) in enumerate(chunks):
        r = jnp.dot(hb, w_ref[:, k * MXU_COLS:(k + 1) * MXU_COLS],
                    preferred_element_type=jnp.float32)
        for half in range(MXU_COLS // LANES):
            j = c * (MXU_COLS // LANES) + half
            y = r[:, half * LANES:(half + 1) * LANES]
            if j in rope_slabs:
                y = (y * cos + pltpu.roll(y, LANES - ROT_DIM // 2, 1) * s1
                     + pltpu.roll(y, ROT_DIM // 2, 1) * s2)
            out_ref[(0,) + slab_dst[j]] = y.astype(jnp.bfloat16)
    if with_forget:
        fa_ref[0] = jnp.dot(hb, wf_ref[...], preferred_element_type=jnp.float32)


def _mod_spec(layer, which):
    return pl.BlockSpec((None, None, None, 1, D_MODEL), lambda bi, i: (layer, which, bi, 0, 0))


def _layer_spec(shape, index):
    return pl.BlockSpec((None,) + tuple(shape[1:]), lambda bi, i: (index,) + (0,) * (len(shape) - 1))


def _in_projection(x, mod, layer, norm_pre, w, w_idx, wf, rope, rope_slabs, slab_dst, residual=None):
    b, s, d = x.shape
    with_forget = wf is not None
    with_residual = residual is not None
    row = lambda bi, i: (bi, i, 0)
    in_specs, args = [], []
    if with_residual:
        o, w_out, out_idx, norm_post, prev = residual
        in_specs += [pl.BlockSpec((1, N_UNITS, TM, LANES), lambda bi, i: (bi, 0, i, 0)),
                     _layer_spec(w_out.shape, out_idx), _mod_spec(prev, 2), _layer_spec(norm_post.shape, prev)]
        args += [o, w_out, mod, norm_post]
    in_specs += [pl.BlockSpec((1, TM, d), row), _mod_spec(layer, 0), _mod_spec(layer, 1),
                 _layer_spec(norm_pre.shape, layer)] + [_layer_spec(part.shape, w_idx) for part in w]
    args += [x, mod, mod, norm_pre] + list(w)
    if with_forget:
        in_specs.append(_layer_spec(wf.shape, w_idx))
        args.append(wf)
    in_specs += [pl.BlockSpec((1, TM, LANES), row)] * 3
    args += list(rope)
    out_specs, out_shape = [], []
    if with_residual:
        out_specs.append(pl.BlockSpec((1, TM, d), row))
        out_shape.append(jax.ShapeDtypeStruct((b, s, d), jnp.float32))
    out_specs.append(pl.BlockSpec((1, N_UNITS, N_KINDS, TM, LANES), lambda bi, i: (bi, 0, 0, i, 0)))
    out_shape.append(jax.ShapeDtypeStruct((b, N_UNITS, N_KINDS, s, LANES), jnp.bfloat16))
    if with_forget:
        out_specs.append(pl.BlockSpec((1, TM, LANES), row))
        out_shape.append(jax.ShapeDtypeStruct((b, s, LANES), jnp.float32))
    res = list(pl.pallas_call(
        functools.partial(_inproj_kernel, rope_slabs=frozenset(rope_slabs), slab_dst=tuple(slab_dst), n_w=len(w),
                          with_forget=with_forget, with_residual=with_residual),
        grid=(b, s // TM),
        in_specs=in_specs,
        out_specs=out_specs,
        out_shape=out_shape,
        compiler_params=_cparams(2),
        name="in_projection",
    )(*args))
    x_new = res.pop(0) if with_residual else x
    proj = res.pop(0)
    fa = res.pop(0) if with_forget else None
    return proj, fa, x_new


F_BLOCK = 256


def _fgate_kernel(fa_ref, b_ref, g_ref):
    s = fa_ref.shape[1]
    nh = N_HEADS_FOX
    z = fa_ref[0] + b_ref[...]
    logf = jnp.minimum(z, 0.0) - jnp.log(1.0 + jnp.exp(-jnp.abs(z)))
    r_i = lax.broadcasted_iota(jnp.int32, (F_BLOCK, F_BLOCK), 0)
    c_i = lax.broadcasted_iota(jnp.int32, (F_BLOCK, F_BLOCK), 1)
    tri = jnp.where(c_i <= r_i, 1.0, 0.0).astype(jnp.bfloat16)
    lane = lax.broadcasted_iota(jnp.int32, (1, LANES), 1)

    def split3(v):
        hi = v.astype(jnp.bfloat16)
        r1 = v - hi.astype(jnp.float32)
        mid = r1.astype(jnp.bfloat16)
        lo = (r1 - mid.astype(jnp.float32)).astype(jnp.bfloat16)
        return hi, mid, lo

    carry = jnp.zeros((1, LANES), jnp.float32)
    for i in range(s // F_BLOCK):
        rows = slice(i * F_BLOCK, (i + 1) * F_BLOCK)
        parts = jnp.concatenate(split3(logf[rows]), axis=1)
        cs3 = jnp.dot(tri, parts, preferred_element_type=jnp.float32)
        f = (cs3[:, :LANES] + cs3[:, LANES:2 * LANES]) + cs3[:, 2 * LANES:] + carry
        carry = f[F_BLOCK - 1:F_BLOCK, :]
        hi, mid, lo = split3(f * LOG2E)
        packed = (jnp.where(lane < nh, hi.astype(jnp.float32), 0.0)
                  + pltpu.roll(jnp.where(lane < nh, mid.astype(jnp.float32), 0.0), nh, 1)
                  + pltpu.roll(jnp.where(lane < nh, lo.astype(jnp.float32), 0.0), 2 * nh, 1))
        ones_hi = jnp.where((lane >= 3 * nh) & (lane < 6 * nh), 1.0, 0.0)
        ones_lo = jnp.where(lane < 3 * nh, 1.0, 0.0)
        g_ref[0, 0, 0, rows, :] = (ones_hi - packed).astype(jnp.bfloat16)
        g_ref[0, 0, 1, rows, :] = (ones_lo + pltpu.roll(packed, 3 * nh, 1)).astype(jnp.bfloat16)
    g_ref[0, 1] = jnp.zeros((2, s, LANES), jnp.bfloat16)


def _forget_prefix(fa, b_forget, idx):
    b, s, _ = fa.shape
    return pl.pallas_call(
        _fgate_kernel,
        grid=(b,),
        in_specs=[pl.BlockSpec((1, s, LANES), lambda i: (i, 0, 0)),
                  pl.BlockSpec((None, 1, LANES), lambda i: (idx, 0, 0))],
        out_specs=pl.BlockSpec((1, 2, 2, s, LANES), lambda i: (i, 0, 0, 0, 0)),
        out_shape=jax.ShapeDtypeStruct((b, 2, 2, s, LANES), jnp.bfloat16),
        compiler_params=_cparams(1),
        name="forget_prefix",
    )(fa, b_forget)


V_ROWS = LANES + 16


KPQ = TQ // TK


def _n_bias_tables(mode, s_len):
    if mode == "even":
        return KPQ
    bounded = max((w // d) * d for (w, d) in DILATED_CONFIGS if (w // d) * d < s_len - 1)
    dist = 0
    while dist * TK - (TK - 1) <= bounded:
        dist += 1
    return dist + KPQ


LOG2_3 = math.log2(3.0)


def _fill_bias_tables(tab_ref, mode, s_len):
    n_tab = tab_ref.shape[0]
    kr = lax.broadcasted_iota(jnp.int32, (TK, TQ), 0)
    qc = lax.broadcasted_iota(jnp.int32, (TK, TQ), 1)
    for idx in range(n_tab):
        delta = (idx - (KPQ - 1)) * TK + qc - kr
        if mode == "even":
            bias = jnp.where(delta >= 0, 0.0, NEG)
        else:
            mult = jnp.zeros((TK, TQ), jnp.int32)
            for (w, d) in DILATED_CONFIGS:
                reach = (w // d) * d
                if idx == n_tab - 1 and reach < s_len - 1:
                    continue
                ok = (delta >= 0) & (jnp.bitwise_and(delta, d - 1) == 0) & (delta <= reach)
                mult = mult + ok.astype(jnp.int32)
            bias = jnp.where(mult == 0, NEG, jnp.where(mult == 1, 0.0, jnp.where(mult == 2, 1.0, LOG2_3)))
        tab_ref[idx] = bias


def _attn_kernel(*refs, mode, lam_init, par_idx):
    if mode == "even":
        (p_ref, aug_ref, lq1_ref, lk1_ref, lq2_ref, lk2_ref, subln_ref, o_ref,
         vt, sb00, sb01, sb10, sb11, cm_buf, m1, a1, m2, a2, tab_ref) = refs
    else:
        (p_ref, o_ref, vt, sb00, sb01, sb10, sb11, cm_buf, m1, a1, m2, a2, tab_ref) = refs
    s_len = p_ref.shape[3]

    @pl.when((pl.program_id(0) == 0) & (pl.program_id(1) == 0))
    def _():
        _fill_bias_tables(tab_ref, mode, s_len)

    def unit(uu, carry):
        u = pl.program_id(1) * UNITS_PER_STEP + uu
        lane = lax.broadcasted_iota(jnp.int32, (1, LANES), 1)
        lo = lane < HEAD_DIM
        nh = N_HEADS_FOX
        nt = (((1,), (1,)), ((), ()))

        vtt = p_ref[0, uu, 2].astype(jnp.float32).T
        ones_rows = jnp.where(lax.broadcasted_iota(jnp.int32, (V_ROWS - LANES, TK), 0) == 0, 1.0, 0.0)
        for j in range(s_len // TK):
            vt[j, 0:LANES, :] = vtt[:, j * TK:(j + 1) * TK].astype(jnp.bfloat16)
            vt[j, LANES:V_ROWS, :] = ones_rows.astype(jnp.bfloat16)

        if mode == "even":
            is_fox = u < N_UNITS // 2

        state = ((m1, a1), (m2, a2))
        n_tab = tab_ref.shape[0]

        def make_lhs(qi):
            q0 = qi * TQ
            qs = p_ref[0, uu, 0, q0:q0 + TQ, :].astype(jnp.float32) * (HEAD_DIM ** -0.5 * LOG2E)
            lhs = [jnp.where(lo, qs, 0.0).astype(jnp.bfloat16),
                   jnp.where(lo, 0.0, qs).astype(jnp.bfloat16)]
            if mode == "even":
                gq = aug_ref[0, 0, 1, q0:q0 + TQ, :].astype(jnp.float32)
                for i in range(2):
                    h = 2 * u + i
                    sel = (lane % nh == h) & (lane < 6 * nh)
                    qa = jnp.where(sel, gq, 0.0)
                    lhs[i] = jnp.concatenate([lhs[i], qa.astype(jnp.bfloat16)], axis=1)
            return lhs

        half = TK // 2
        assert TQ == TK
        s_bufs = ((sb00, sb01), (sb10, sb11))
        z0 = pl.multiple_of(jnp.minimum(pl.program_id(0), 0), TK)

        def scores_into(slot, lhs, qi, kt):
            k0 = kt * TK

            def keys(r0, n):
                kb = p_ref[0, uu, 1, r0:r0 + n, :]
                if mode == "even":
                    kb = jnp.concatenate([kb, aug_ref[0, 0, 0, r0:r0 + n, :]], axis=1)
                return kb

            dist = qi - kt
            for i in range(2):
                if dist > 0:
                    st = lax.dot_general(keys(k0, TK), lhs[i], nt,
                                         preferred_element_type=jnp.float32)
                    if mode != "even":
                        st = st + tab_ref[min(dist, n_tab - 1)]
                    s_bufs[slot][i][pl.ds(z0, TK), :] = st
                    cm_buf[slot, i] = jnp.max(st, axis=0, keepdims=True)
                else:
                    sa = lax.dot_general(keys(k0, half), lhs[i], nt,
                                         preferred_element_type=jnp.float32) + tab_ref[0, 0:half, :]
                    sb = lax.dot_general(keys(k0 + half, half), lhs[i][half:, :], nt,
                                         preferred_element_type=jnp.float32) + tab_ref[0, half:, half:]
                    s_bufs[slot][i][pl.ds(z0, half), :] = sa
                    s_bufs[slot][i][pl.ds(z0 + half, half), half:] = sb
                    ca = jnp.max(sa, axis=0, keepdims=True)
                    cb = jnp.max(sb, axis=0, keepdims=True)
                    cm_buf[slot, i] = jnp.concatenate(
                        [ca[:, :half], jnp.maximum(ca[:, half:], cb)], axis=1)

        def consume(slot, qi, kt):
            for i in range(2):
                m_r, a_r = state[i]
                m_prev = m_r[...]
                m_next = jnp.maximum(m_prev, cm_buf[slot, i])
                alpha = jnp.exp2(m_prev - m_next)
                if qi > kt:
                    pt = jnp.exp2(s_bufs[slot][i][pl.ds(z0, TK), :] - m_next).astype(jnp.bfloat16)
                    a_r[...] = alpha * a_r[...] + jnp.dot(vt[kt], pt, preferred_element_type=jnp.float32)
                else:
                    pa = jnp.exp2(s_bufs[slot][i][pl.ds(z0, half), :] - m_next).astype(jnp.bfloat16)
                    pb = jnp.exp2(s_bufs[slot][i][pl.ds(z0 + half, half), half:]
                                  - m_next[:, half:]).astype(jnp.bfloat16)
                    pva = jnp.dot(vt[kt, :, 0:half], pa, preferred_element_type=jnp.float32)
                    pvb = jnp.dot(vt[kt, :, half:], pb, preferred_element_type=jnp.float32)
                    a_r[:, 0:half] = alpha[:, :half] * a_r[:, 0:half] + pva[:, :half]
                    a_r[:, half:] = alpha[:, half:] * a_r[:, half:] + (pva[:, half:] + pvb)
                m_r[...] = m_next

        def finish(qi):
            q0 = qi * TQ
            outs = []
            for (m_r, a_r) in state:
                acc = a_r[...]
                inv_l = 1.0 / acc[LANES:LANES + 1, :]
                outs.append((acc[0:LANES, :] * inv_l).T)
            o1, o2 = outs
            g = p_ref[0, uu, 3, q0:q0 + TQ, :].astype(jnp.float32)
            gate = g * jax.nn.sigmoid(g)
            out = jnp.where(lo, o1, o2)
            if mode == "even":
                pr = slice(par_idx, par_idx + 1)
                lam = (jnp.exp(jnp.sum(lq1_ref[pr, :] * lk1_ref[pr, :], axis=1, keepdims=True))
                       - jnp.exp(jnp.sum(lq2_ref[pr, :] * lk2_ref[pr, :], axis=1, keepdims=True)) + lam_init)
                d = o1 - lam * o2
                var = jnp.mean(d * d, axis=-1, keepdims=True)
                dn = (d * lax.rsqrt(var + NORM_EPS) * subln_ref[pr, :]) * (1.0 - lam_init)
                out = jnp.where(is_fox, out, dn)
            o_ref[0, uu, q0:q0 + TQ, :] = (out * gate).astype(jnp.bfloat16)

        pairs = [(qi, kt) for qi in range(s_len // TQ) for kt in range(qi + 1)]
        lhs = make_lhs(0)
        scores_into(0, lhs, 0, 0)
        for n, (qi, kt) in enumerate(pairs):
            if kt == 0:
                for (m_r, a_r) in state:
                    m_r[...] = jnp.full((1, TQ), NEG, jnp.float32)
                    a_r[...] = jnp.zeros((V_ROWS, TQ), jnp.float32)
            if n + 1 < len(pairs):
                nqi, nkt = pairs[n + 1]
                if nkt == 0:
                    lhs = make_lhs(nqi)
                scores_into((n + 1) % 2, lhs, nqi, nkt)
            consume(n % 2, qi, kt)
            if kt == qi:
                finish(qi)
        return carry

    lax.fori_loop(0, UNITS_PER_STEP, unit, 0)


def _attention(proj, mode, aug=None, params=None, par_idx=0, lam_init=0.0):
    b, _, _, s, _ = proj.shape
    in_specs = [pl.BlockSpec((1, UNITS_PER_STEP, N_KINDS, s, LANES), lambda bi, p: (bi, p, 0, 0, 0))]
    args = [proj]
    scratch = []
    if mode == "even":
        in_specs.append(pl.BlockSpec((1, 1, 2, s, LANES),
                                     lambda bi, p: (bi, p * UNITS_PER_STEP // (N_UNITS // 2), 0, 0, 0)))
        args.append(aug)
    if mode == "even":
        for arr in params:
            in_specs.append(pl.BlockSpec(arr.shape, lambda bi, p: (0, 0)))
            args.append(arr)
    scratch.append(pltpu.VMEM((s // TK, V_ROWS, TK), jnp.bfloat16))
    scratch += [pltpu.VMEM((TK, TQ), jnp.float32)] * 4
    scratch.append(pltpu.VMEM((2, 2, 1, TQ), jnp.float32))
    scratch += [pltpu.VMEM((1, TQ), jnp.float32), pltpu.VMEM((V_ROWS, TQ), jnp.float32)] * 2
    scratch.append(pltpu.VMEM((_n_bias_tables(mode, s), TK, TQ), jnp.float32))
    return pl.pallas_call(
        functools.partial(_attn_kernel, mode=mode, lam_init=lam_init, par_idx=par_idx),
        grid=(b, N_UNITS // UNITS_PER_STEP),
        in_specs=in_specs,
        out_specs=pl.BlockSpec((1, UNITS_PER_STEP, s, LANES), lambda bi, p: (bi, p, 0, 0)),
        out_shape=jax.ShapeDtypeStruct((b, N_UNITS, s, LANES), jnp.bfloat16),
        scratch_shapes=scratch,
        compiler_params=_cparams(2),
        name="attention_" + mode,
    )(*args)


def _outproj_kernel(o_ref, w_ref, x_ref, gate_ref, npost_ref, out_ref):
    out_ref[0] = _residual_update(o_ref, w_ref, x_ref[0], gate_ref[...], npost_ref[...])


def _out_projection(o, w_out, out_idx, x, mod, layer, norm_post):
    b, s, d = x.shape
    return pl.pallas_call(
        _outproj_kernel,
        grid=(b, s // TM),
        in_specs=[
            pl.BlockSpec((1, N_UNITS, TM, LANES), lambda bi, i: (bi, 0, i, 0)),
            _layer_spec(w_out.shape, out_idx),
            pl.BlockSpec((1, TM, d), lambda bi, i: (bi, i, 0)),
            _mod_spec(layer, 2),
            _layer_spec(norm_post.shape, layer),
        ],
        out_specs=pl.BlockSpec((1, TM, d), lambda bi, i: (bi, i, 0)),
        out_shape=jax.ShapeDtypeStruct((b, s, d), jnp.float32),
        compiler_params=_cparams(2),
        name="out_projection",
    )(o, w_out, x, mod, norm_post)


def _even_weight_layout(w_in):
    f0 = 3 * N_HEADS_FOX * HEAD_DIM
    f1 = f0 + N_HEADS_FOX
    parts = (w_in[..., :f0].astype(jnp.bfloat16), w_in[..., f1:].astype(jnp.bfloat16))
    wf = jnp.pad(w_in[..., f0:f1], ((0, 0), (0, 0), (0, LANES - N_HEADS_FOX))).astype(jnp.bfloat16)
    return parts, wf


def kernel(x, c, positions, norm_pre, norm_post, ada_w, ada_b, ev_w_in, ev_b_forget,
           ev_lambda_q1, ev_lambda_k1, ev_lambda_q2, ev_lambda_k2, ev_subln, ev_w_out,
           od_w_in, od_w_out):
    depth, d = norm_pre.shape
    mod = _adaln_mod(c, ada_w, ada_b)
    rope = _rope_tables(positions)
    half = N_UNITS // 2
    even_dst = ([(j % half, j // half) for j in range(N_KINDS * half)]
                + [(half + j % half, j // half) for j in range(N_KINDS * half)])
    odd_dst = [(j % N_UNITS, j // N_UNITS) for j in range(N_SLABS)]
    even_rope = set(range(4 * half, 6 * half))
    odd_rope = set(range(2 * N_UNITS))
    ev_w_main, ev_wf = _even_weight_layout(ev_w_in)
    od_w_main = (od_w_in.astype(jnp.bfloat16),)
    w_outs = (ev_w_out.astype(jnp.bfloat16), od_w_out.astype(jnp.bfloat16))
    npre = norm_pre.reshape(depth, 1, d)
    npost = norm_post.reshape(depth, 1, d)
    b_forget = jnp.pad(ev_b_forget, ((0, 0), (0, LANES - N_HEADS_FOX))).reshape(-1, 1, LANES)
    lam_params = (ev_lambda_q1, ev_lambda_k1, ev_lambda_q2, ev_lambda_k2, ev_subln)
    pending = None
    for layer in range(depth):
        idx = layer // 2
        if layer % 2 == 0:
            proj, fa, x = _in_projection(x, mod, layer, npre, ev_w_main, idx, ev_wf, rope, even_rope, even_dst,
                                         pending)
            aug = _forget_prefix(fa, b_forget, idx)
            lam_init = 0.8 - 0.6 * math.exp(-0.3 * layer)
            o = _attention(proj, "even", aug=aug, params=lam_params, par_idx=idx, lam_init=lam_init)
        else:
            proj, _, x = _in_projection(x, mod, layer, npre, od_w_main, idx, None, rope, odd_rope, odd_dst,
                                        pending)
            o = _attention(proj, "odd")
        pending = (o, w_outs[layer % 2], idx, npost, layer)
    o, w_out, idx, _, layer = pending
    return _out_projection(o, w_out, idx, x, mod, layer, npost)
```

```python
import functools
import math

import jax
import jax.numpy as jnp
from jax import lax
from jax.experimental import pallas as pl
from jax.experimental.pallas import tpu as pltpu

D_MODEL = 1024
HEAD_DIM = 64
N_HEADS_FOX = 8
ROT_DIM = HEAD_DIM // 4
ROPE_THETA = 500000.0
NORM_EPS = 1e-6
DILATED_CONFIGS = ((128, 1), (512, 4), (2048, 16))

LANES = 128
MXU_COLS = 256
N_SLABS = 32
N_UNITS = 8
N_KINDS = 4
UNITS_PER_STEP = 1
TQ = 512
TK = 512
TM = 512
NEG = -1e30
LOG2E = math.log2(math.e)
VMEM_LIMIT = 48 * 1024 * 1024


def _cparams(n_axes):
    return pltpu.CompilerParams(
        dimension_semantics=("arbitrary",) * n_axes, vmem_limit_bytes=VMEM_LIMIT)


def _mod_kernel(c_ref, w_ref, b_ref, o_ref):
    c = c_ref[...]
    cond = c * jax.nn.sigmoid(c)
    for j in range(3):
        o_ref[0, j] = (jnp.dot(cond, w_ref[0, :, j * D_MODEL:(j + 1) * D_MODEL],
                               preferred_element_type=jnp.float32) + b_ref[0, j])


def _adaln_mod(c, ada_w, ada_b):
    depth = ada_w.shape[0]
    b = c.shape[0]
    out = pl.pallas_call(
        _mod_kernel,
        grid=(depth,),
        in_specs=[
            pl.BlockSpec((b, D_MODEL), lambda l: (0, 0)),
            pl.BlockSpec((1, D_MODEL, 3 * D_MODEL), lambda l: (l, 0, 0)),
            pl.BlockSpec((1, 3, 1, D_MODEL), lambda l: (l, 0, 0, 0)),
        ],
        out_specs=pl.BlockSpec((1, 3, b, D_MODEL), lambda l: (l, 0, 0, 0)),
        out_shape=jax.ShapeDtypeStruct((depth, 3, b, D_MODEL), jnp.float32),
        compiler_params=_cparams(1),
        name="adaln_mod",
    )(c, ada_w, ada_b.reshape(depth, 3, 1, D_MODEL))
    return out.reshape(depth, 3, b, 1, D_MODEL)


ROPE_PACK = LANES // (ROT_DIM // 2)


def _rope_kernel(pos_ref, c_ref, s1_ref, s2_ref):
    nf = ROT_DIM // 2
    pos = pos_ref[0].astype(jnp.float32)
    lane = lax.broadcasted_iota(jnp.int32, (1, LANES), 1)
    fidx = (lane % nf).astype(jnp.float32)
    theta = jnp.full((1, LANES), ROPE_THETA, jnp.float32)
    inv_freq = jnp.exp(-(fidx * (2.0 / ROT_DIM)) * jnp.log(theta))
    ang = pos * inv_freq
    n_rows = pos.shape[0]

    def split3(v):
        hi = v.astype(jnp.bfloat16)
        r1 = v - hi.astype(jnp.float32)
        mid = r1.astype(jnp.bfloat16)
        lo = (r1 - mid.astype(jnp.float32)).astype(jnp.bfloat16)
        return jnp.concatenate([hi, mid, lo], axis=0)

    cos3 = split3(jnp.cos(ang))
    sin3 = split3(jnp.sin(ang))
    src = lax.broadcasted_iota(jnp.int32, (LANES, LANES), 0)
    dst = lax.broadcasted_iota(jnp.int32, (LANES, LANES), 1)
    cc = dst % HEAD_DIM
    for j in range(ROPE_PACK):
        hit = (src // nf == j) & (src % nf == cc % nf)
        e_c = jnp.where(hit & (cc < ROT_DIM), 1.0, 0.0).astype(jnp.bfloat16)
        e_1 = jnp.where(hit & (cc < nf), -1.0, 0.0).astype(jnp.bfloat16)
        e_2 = jnp.where(hit & (cc >= nf) & (cc < ROT_DIM), 1.0, 0.0).astype(jnp.bfloat16)
        rows = pl.ds(j, n_rows, stride=ROPE_PACK)
        for ref, tab3, e, base in ((c_ref, cos3, e_c, 1.0), (s1_ref, sin3, e_1, 0.0), (s2_ref, sin3, e_2, 0.0)):
            t3 = jnp.dot(tab3, e, preferred_element_type=jnp.float32)
            t = (t3[0:n_rows] + t3[n_rows:2 * n_rows]) + t3[2 * n_rows:]
            if base:
                t = t + jnp.where(dst[0:1] % HEAD_DIM < ROT_DIM, 0.0, base)
            ref[0, rows, :] = t


def _rope_tables(positions):
    b, s = positions.shape
    packed = jnp.repeat(positions.reshape(b, s // ROPE_PACK, ROPE_PACK), ROT_DIM // 2, axis=-1)
    spec = pl.BlockSpec((1, s, LANES), lambda i: (i, 0, 0))
    shp = jax.ShapeDtypeStruct((b, s, LANES), jnp.float32)
    return pl.pallas_call(
        _rope_kernel,
        grid=(b,),
        in_specs=[pl.BlockSpec((1, s // ROPE_PACK, LANES), lambda i: (i, 0, 0))],
        out_specs=[spec, spec, spec],
        out_shape=[shp, shp, shp],
        compiler_params=_cparams(1),
        name="rope_tables",
    )(packed)


def _residual_update(o_ref, wout_ref, rows, x, gate, npost):
    cat = jnp.concatenate([o_ref[0, j, rows, :] for j in range(N_UNITS)], axis=1)
    y = jnp.dot(cat, wout_ref[...], preferred_element_type=jnp.float32)
    var = jnp.mean(y * y, axis=-1, keepdims=True)
    return x + gate * (y * lax.rsqrt(var + NORM_EPS) * npost)


SUB_TILES = 2


def _inproj_kernel(*refs, rope_slabs, slab_dst, n_w, with_forget, with_residual):
    refs = list(refs)
    if with_residual:
        o_ref, wout_ref, gate_ref, npost_ref = refs[:4]
        refs = refs[4:]
    x_ref, shift_ref, scale_ref, npre_ref = refs[:4]
    w_refs, refs = refs[4:4 + n_w], refs[4 + n_w:]
    wf_ref = refs.pop(0) if with_forget else None
    c_ref, s1_ref, s2_ref = refs[:3]
    refs = refs[3:]
    xout_ref = refs.pop(0) if with_residual else None
    out_ref = refs.pop(0)
    fa_ref = refs.pop(0) if with_forget else None

    tm = x_ref.shape[1]
    sub_rows = [slice(t * tm // SUB_TILES, (t + 1) * tm // SUB_TILES) for t in range(SUB_TILES)]
    hbs = []
    for rows in sub_rows:
        x = x_ref[0, rows, :]
        if with_residual:
            x = _residual_update(o_ref, wout_ref, rows, x, gate_ref[...], npost_ref[...])
            xout_ref[0, rows, :] = x
        var = jnp.mean(x * x, axis=-1, keepdims=True)
        h = (x * lax.rsqrt(var + NORM_EPS) * npre_ref[...]) * (1.0 + scale_ref[...]) + shift_ref[...]
        hbs.append(h.astype(jnp.bfloat16))
    chunks = [(w_ref, k) for w_ref in w_refs for k in range(w_ref.shape[1] // MXU_COLS)]
    assert len(chunks) == N_SLABS * LANES // MXU_COLS
    for rows, hb in zip(sub_rows, hbs):
        cos = c_ref[0, rows, :]
        s1 = s1_ref[0, rows, :]
        s2 = s2_ref[0, rows, :]
        for c, (w_ref, k) in enumerate(chunks):
            r = jnp.dot(hb, w_ref[:, k * MXU_COLS:(k + 1) * MXU_COLS],
                        preferred_element_type=jnp.float32)
            for half in range(MXU_COLS // LANES):
                j = c * (MXU_COLS // LANES) + half
                y = r[:, half * LANES:(half + 1) * LANES]
                if j in rope_slabs:
                    y = (y * cos + pltpu.roll(y, LANES - ROT_DIM // 2, 1) * s1
                         + pltpu.roll(y, ROT_DIM // 2, 1) * s2)
                out_ref[(0,) + slab_dst[j] + (rows, slice(None))] = y.astype(jnp.bfloat16)
        if with_forget:
            fa_ref[0, rows, :] = jnp.dot(hb, wf_ref[...], preferred_element_type=jnp.float32)


def _mod_spec(layer, which):
    return pl.BlockSpec((None, None, None, 1, D_MODEL), lambda bi, i: (layer, which, bi, 0, 0))


def _layer_spec(shape, index):
    return pl.BlockSpec((None,) + tuple(shape[1:]), lambda bi, i: (index,) + (0,) * (len(shape) - 1))


def _in_projection(x, mod, layer, norm_pre, w, w_idx, wf, rope, rope_slabs, slab_dst, residual=None):
    b, s, d = x.shape
    with_forget = wf is not None
    with_residual = residual is not None
    row = lambda bi, i: (bi, i, 0)
    in_specs, args = [], []
    if with_residual:
        o, w_out, out_idx, norm_post, prev = residual
        in_specs += [pl.BlockSpec((1, N_UNITS, TM, LANES), lambda bi, i: (bi, 0, i, 0)),
                     _layer_spec(w_out.shape, out_idx), _mod_spec(prev, 2), _layer_spec(norm_post.shape, prev)]
        args += [o, w_out, mod, norm_post]
    in_specs += [pl.BlockSpec((1, TM, d), row), _mod_spec(layer, 0), _mod_spec(layer, 1),
                 _layer_spec(norm_pre.shape, layer)] + [_layer_spec(part.shape, w_idx) for part in w]
    args += [x, mod, mod, norm_pre] + list(w)
    if with_forget:
        in_specs.append(_layer_spec(wf.shape, w_idx))
        args.append(wf)
    in_specs += [pl.BlockSpec((1, TM, LANES), row)] * 3
    args += list(rope)
    out_specs, out_shape = [], []
    if with_residual:
        out_specs.append(pl.BlockSpec((1, TM, d), row))
        out_shape.append(jax.ShapeDtypeStruct((b, s, d), jnp.float32))
    out_specs.append(pl.BlockSpec((1, N_UNITS, N_KINDS, TM, LANES), lambda bi, i: (bi, 0, 0, i, 0)))
    out_shape.append(jax.ShapeDtypeStruct((b, N_UNITS, N_KINDS, s, LANES), jnp.bfloat16))
    if with_forget:
        out_specs.append(pl.BlockSpec((1, TM, LANES), row))
        out_shape.append(jax.ShapeDtypeStruct((b, s, LANES), jnp.float32))
    res = list(pl.pallas_call(
        functools.partial(_inproj_kernel, rope_slabs=frozenset(rope_slabs), slab_dst=tuple(slab_dst), n_w=len(w),
                          with_forget=with_forget, with_residual=with_residual),
        grid=(b, s // TM),
        in_specs=in_specs,
        out_specs=out_specs,
        out_shape=out_shape,
        compiler_params=_cparams(2),
        name="in_projection",
    )(*args))
    x_new = res.pop(0) if with_residual else x
    proj = res.pop(0)
    fa = res.pop(0) if with_forget else None
    return proj, fa, x_new


F_BLOCK = 256


def _fgate_kernel(fa_ref, b_ref, g_ref):
    s = fa_ref.shape[1]
    nh = N_HEADS_FOX
    z = fa_ref[0] + b_ref[...]
    logf = jnp.minimum(z, 0.0) - jnp.log(1.0 + jnp.exp(-jnp.abs(z)))
    r_i = lax.broadcasted_iota(jnp.int32, (F_BLOCK, F_BLOCK), 0)
    c_i = lax.broadcasted_iota(jnp.int32, (F_BLOCK, F_BLOCK), 1)
    tri = jnp.where(c_i <= r_i, 1.0, 0.0).astype(jnp.bfloat16)
    lane = lax.broadcasted_iota(jnp.int32, (1, LANES), 1)

    def split3(v):
        hi = v.astype(jnp.bfloat16)
        r1 = v - hi.astype(jnp.float32)
        mid = r1.astype(jnp.bfloat16)
        lo = (r1 - mid.astype(jnp.float32)).astype(jnp.bfloat16)
        return hi, mid, lo

    carry = jnp.zeros((1, LANES), jnp.float32)
    for i in range(s // F_BLOCK):
        rows = slice(i * F_BLOCK, (i + 1) * F_BLOCK)
        parts = jnp.concatenate(split3(logf[rows]), axis=1)
        cs3 = jnp.dot(tri, parts, preferred_element_type=jnp.float32)
        f = (cs3[:, :LANES] + cs3[:, LANES:2 * LANES]) + cs3[:, 2 * LANES:] + carry
        carry = f[F_BLOCK - 1:F_BLOCK, :]
        hi, mid, lo = split3(f * LOG2E)
        packed = (jnp.where(lane < nh, hi.astype(jnp.float32), 0.0)
                  + pltpu.roll(jnp.where(lane < nh, mid.astype(jnp.float32), 0.0), nh, 1)
                  + pltpu.roll(jnp.where(lane < nh, lo.astype(jnp.float32), 0.0), 2 * nh, 1))
        ones_hi = jnp.where((lane >= 3 * nh) & (lane < 6 * nh), 1.0, 0.0)
        ones_lo = jnp.where(lane < 3 * nh, 1.0, 0.0)
        g_ref[0, 0, 0, rows, :] = (ones_hi - packed).astype(jnp.bfloat16)
        g_ref[0, 0, 1, rows, :] = (ones_lo + pltpu.roll(packed, 3 * nh, 1)).astype(jnp.bfloat16)
    g_ref[0, 1] = jnp.zeros((2, s, LANES), jnp.bfloat16)


def _forget_prefix(fa, b_forget, idx):
    b, s, _ = fa.shape
    return pl.pallas_call(
        _fgate_kernel,
        grid=(b,),
        in_specs=[pl.BlockSpec((1, s, LANES), lambda i: (i, 0, 0)),
                  pl.BlockSpec((None, 1, LANES), lambda i: (idx, 0, 0))],
        out_specs=pl.BlockSpec((1, 2, 2, s, LANES), lambda i: (i, 0, 0, 0, 0)),
        out_shape=jax.ShapeDtypeStruct((b, 2, 2, s, LANES), jnp.bfloat16),
        compiler_params=_cparams(1),
        name="forget_prefix",
    )(fa, b_forget)


V_ROWS = LANES + 16


KPQ = TQ // TK


def _n_bias_tables(mode, s_len):
    if mode == "even":
        return KPQ
    bounded = max((w // d) * d for (w, d) in DILATED_CONFIGS if (w // d) * d < s_len - 1)
    dist = 0
    while dist * TK - (TK - 1) <= bounded:
        dist += 1
    return dist + KPQ


LOG2_3 = math.log2(3.0)


def _fill_bias_tables(tab_ref, mode, s_len):
    n_tab = tab_ref.shape[0]
    kr = lax.broadcasted_iota(jnp.int32, (TK, TQ), 0)
    qc = lax.broadcasted_iota(jnp.int32, (TK, TQ), 1)
    for idx in range(n_tab):
        delta = (idx - (KPQ - 1)) * TK + qc - kr
        if mode == "even":
            bias = jnp.where(delta >= 0, 0.0, NEG)
        else:
            mult = jnp.zeros((TK, TQ), jnp.int32)
            for (w, d) in DILATED_CONFIGS:
                reach = (w // d) * d
                if idx == n_tab - 1 and reach < s_len - 1:
                    continue
                ok = (delta >= 0) & (jnp.bitwise_and(delta, d - 1) == 0) & (delta <= reach)
                mult = mult + ok.astype(jnp.int32)
            bias = jnp.where(mult == 0, NEG, jnp.where(mult == 1, 0.0, jnp.where(mult == 2, 1.0, LOG2_3)))
        tab_ref[idx] = bias


def _attn_kernel(*refs, mode, lam_init, par_idx):
    if mode == "even":
        (p_ref, aug_ref, lq1_ref, lk1_ref, lq2_ref, lk2_ref, subln_ref, o_ref,
         vt, sb00, sb01, sb10, sb11, cm_buf, m1, a1, m2, a2, tab_ref) = refs
    else:
        (p_ref, o_ref, vt, sb00, sb01, sb10, sb11, cm_buf, m1, a1, m2, a2, tab_ref) = refs
    s_len = p_ref.shape[3]

    @pl.when((pl.program_id(0) == 0) & (pl.program_id(1) == 0))
    def _():
        _fill_bias_tables(tab_ref, mode, s_len)

    def unit(uu, carry):
        u = pl.program_id(1) * UNITS_PER_STEP + uu
        lane = lax.broadcasted_iota(jnp.int32, (1, LANES), 1)
        lo = lane < HEAD_DIM
        nh = N_HEADS_FOX
        nt = (((1,), (1,)), ((), ()))

        vtt = p_ref[0, uu, 2].astype(jnp.float32).T
        ones_rows = jnp.where(lax.broadcasted_iota(jnp.int32, (V_ROWS - LANES, TK), 0) == 0, 1.0, 0.0)
        for j in range(s_len // TK):
            vt[j, 0:LANES, :] = vtt[:, j * TK:(j + 1) * TK].astype(jnp.bfloat16)
            vt[j, LANES:V_ROWS, :] = ones_rows.astype(jnp.bfloat16)

        if mode == "even":
            is_fox = u < N_UNITS // 2

        state = ((m1, a1), (m2, a2))
        n_tab = tab_ref.shape[0]

        def make_lhs(qi):
            q0 = qi * TQ
            qs = p_ref[0, uu, 0, q0:q0 + TQ, :].astype(jnp.float32) * (HEAD_DIM ** -0.5 * LOG2E)
            lhs = [jnp.where(lo, qs, 0.0).astype(jnp.bfloat16),
                   jnp.where(lo, 0.0, qs).astype(jnp.bfloat16)]
            if mode == "even":
                gq = aug_ref[0, 0, 1, q0:q0 + TQ, :].astype(jnp.float32)
                for i in range(2):
                    h = 2 * u + i
                    sel = (lane % nh == h) & (lane < 6 * nh)
                    qa = jnp.where(sel, gq, 0.0)
                    lhs[i] = jnp.concatenate([lhs[i], qa.astype(jnp.bfloat16)], axis=1)
            return lhs

        half = TK // 2
        assert TQ == TK
        s_bufs = ((sb00, sb01), (sb10, sb11))
        z0 = pl.multiple_of(jnp.minimum(pl.program_id(0), 0), TK)

        def scores_into(slot, lhs, qi, kt):
            k0 = kt * TK

            def keys(r0, n):
                kb = p_ref[0, uu, 1, r0:r0 + n, :]
                if mode == "even":
                    kb = jnp.concatenate([kb, aug_ref[0, 0, 0, r0:r0 + n, :]], axis=1)
                return kb

            dist = qi - kt
            for i in range(2):
                if dist > 0:
                    st = lax.dot_general(keys(k0, TK), lhs[i], nt,
                                         preferred_element_type=jnp.float32)
                    if mode != "even":
                        st = st + tab_ref[min(dist, n_tab - 1)]
                    s_bufs[slot][i][pl.ds(z0, TK), :] = st
                    cm_buf[slot, i] = jnp.max(st, axis=0, keepdims=True)
                else:
                    sa = lax.dot_general(keys(k0, half), lhs[i], nt,
                                         preferred_element_type=jnp.float32) + tab_ref[0, 0:half, :]
                    sb = lax.dot_general(keys(k0 + half, half), lhs[i][half:, :], nt,
                                         preferred_element_type=jnp.float32) + tab_ref[0, half:, half:]
                    s_bufs[slot][i][pl.ds(z0, half), :] = sa
                    s_bufs[slot][i][pl.ds(z0 + half, half), half:] = sb
                    ca = jnp.max(sa, axis=0, keepdims=True)
                    cb = jnp.max(sb, axis=0, keepdims=True)
                    cm_buf[slot, i] = jnp.concatenate(
                        [ca[:, :half], jnp.maximum(ca[:, half:], cb)], axis=1)

        def consume(slot, qi, kt):
            for i in range(2):
                m_r, a_r = state[i]
                m_prev = m_r[...]
                m_next = jnp.maximum(m_prev, cm_buf[slot, i])
                alpha = jnp.exp2(m_prev - m_next)
                if qi > kt:
                    pt = jnp.exp2(s_bufs[slot][i][pl.ds(z0, TK), :] - m_next).astype(jnp.bfloat16)
                    a_r[...] = alpha * a_r[...] + jnp.dot(vt[kt], pt, preferred_element_type=jnp.float32)
                else:
                    pa = jnp.exp2(s_bufs[slot][i][pl.ds(z0, half), :] - m_next).astype(jnp.bfloat16)
                    pb = jnp.exp2(s_bufs[slot][i][pl.ds(z0 + half, half), half:]
                                  - m_next[:, half:]).astype(jnp.bfloat16)
                    pva = jnp.dot(vt[kt, :, 0:half], pa, preferred_element_type=jnp.float32)
                    pvb = jnp.dot(vt[kt, :, half:], pb, preferred_element_type=jnp.float32)
                    a_r[:, 0:half] = alpha[:, :half] * a_r[:, 0:half] + pva[:, :half]
                    a_r[:, half:] = alpha[:, half:] * a_r[:, half:] + (pva[:, half:] + pvb)
                m_r[...] = m_next

        def finish(qi):
            q0 = qi * TQ
            outs = []
            for (m_r, a_r) in state:
                acc = a_r[...]
                inv_l = 1.0 / acc[LANES:LANES + 1, :]
                outs.append((acc[0:LANES, :] * inv_l).T)
            o1, o2 = outs
            g = p_ref[0, uu, 3, q0:q0 + TQ, :].astype(jnp.float32)
            gate = g * jax.nn.sigmoid(g)
            out = jnp.where(lo, o1, o2)
            if mode == "even":
                pr = slice(par_idx, par_idx + 1)
                lam = (jnp.exp(jnp.sum(lq1_ref[pr, :] * lk1_ref[pr, :], axis=1, keepdims=True))
                       - jnp.exp(jnp.sum(lq2_ref[pr, :] * lk2_ref[pr, :], axis=1, keepdims=True)) + lam_init)
                d = o1 - lam * o2
                var = jnp.mean(d * d, axis=-1, keepdims=True)
                dn = (d * lax.rsqrt(var + NORM_EPS) * subln_ref[pr, :]) * (1.0 - lam_init)
                out = jnp.where(is_fox, out, dn)
            o_ref[0, uu, q0:q0 + TQ, :] = (out * gate).astype(jnp.bfloat16)

        pairs = [(qi, kt) for qi in range(s_len // TQ) for kt in range(qi + 1)]
        lhs = make_lhs(0)
        scores_into(0, lhs, 0, 0)
        for n, (qi, kt) in enumerate(pairs):
            if kt == 0:
                for (m_r, a_r) in state:
                    m_r[...] = jnp.full((1, TQ), NEG, jnp.float32)
                    a_r[...] = jnp.zeros((V_ROWS, TQ), jnp.float32)
            if n + 1 < len(pairs):
                nqi, nkt = pairs[n + 1]
                if nkt == 0:
                    lhs = make_lhs(nqi)
                scores_into((n + 1) % 2, lhs, nqi, nkt)
            consume(n % 2, qi, kt)
            if kt == qi:
                finish(qi)
        return carry

    lax.fori_loop(0, UNITS_PER_STEP, unit, 0)


def _attention(proj, mode, aug=None, params=None, par_idx=0, lam_init=0.0):
    b, _, _, s, _ = proj.shape
    in_specs = [pl.BlockSpec((1, UNITS_PER_STEP, N_KINDS, s, LANES), lambda bi, p: (bi, p, 0, 0, 0))]
    args = [proj]
    scratch = []
    if mode == "even":
        in_specs.append(pl.BlockSpec((1, 1, 2, s, LANES),
                                     lambda bi, p: (bi, p * UNITS_PER_STEP // (N_UNITS // 2), 0, 0, 0)))
        args.append(aug)
    if mode == "even":
        for arr in params:
            in_specs.append(pl.BlockSpec(arr.shape, lambda bi, p: (0, 0)))
            args.append(arr)
    scratch.append(pltpu.VMEM((s // TK, V_ROWS, TK), jnp.bfloat16))
    scratch += [pltpu.VMEM((TK, TQ), jnp.float32)] * 4
    scratch.append(pltpu.VMEM((2, 2, 1, TQ), jnp.float32))
    scratch += [pltpu.VMEM((1, TQ), jnp.float32), pltpu.VMEM((V_ROWS, TQ), jnp.float32)] * 2
    scratch.append(pltpu.VMEM((_n_bias_tables(mode, s), TK, TQ), jnp.float32))
    return pl.pallas_call(
        functools.partial(_attn_kernel, mode=mode, lam_init=lam_init, par_idx=par_idx),
        grid=(b, N_UNITS // UNITS_PER_STEP),
        in_specs=in_specs,
        out_specs=pl.BlockSpec((1, UNITS_PER_STEP, s, LANES), lambda bi, p: (bi, p, 0, 0)),
        out_shape=jax.ShapeDtypeStruct((b, N_UNITS, s, LANES), jnp.bfloat16),
        scratch_shapes=scratch,
        compiler_params=_cparams(2),
        name="attention_" + mode,
    )(*args)


def _outproj_kernel(o_ref, w_ref, x_ref, gate_ref, npost_ref, out_ref):
    out_ref[0] = _residual_update(o_ref, w_ref, slice(None), x_ref[0], gate_ref[...], npost_ref[...])


def _out_projection(o, w_out, out_idx, x, mod, layer, norm_post):
    b, s, d = x.shape
    return pl.pallas_call(
        _outproj_kernel,
        grid=(b, s // TM),
        in_specs=[
            pl.BlockSpec((1, N_UNITS, TM, LANES), lambda bi, i: (bi, 0, i, 0)),
            _layer_spec(w_out.shape, out_idx),
            pl.BlockSpec((1, TM, d), lambda bi, i: (bi, i, 0)),
            _mod_spec(layer, 2),
            _layer_spec(norm_post.shape, layer),
        ],
        out_specs=pl.BlockSpec((1, TM, d), lambda bi, i: (bi, i, 0)),
        out_shape=jax.ShapeDtypeStruct((b, s, d), jnp.float32),
        compiler_params=_cparams(2),
        name="out_projection",
    )(o, w_out, x, mod, norm_post)


def _even_weight_layout(w_in):
    f0 = 3 * N_HEADS_FOX * HEAD_DIM
    f1 = f0 + N_HEADS_FOX
    parts = (w_in[..., :f0].astype(jnp.bfloat16), w_in[..., f1:].astype(jnp.bfloat16))
    wf = jnp.pad(w_in[..., f0:f1], ((0, 0), (0, 0), (0, LANES - N_HEADS_FOX))).astype(jnp.bfloat16)
    return parts, wf


def kernel(x, c, positions, norm_pre, norm_post, ada_w, ada_b, ev_w_in, ev_b_forget,
           ev_lambda_q1, ev_lambda_k1, ev_lambda_q2, ev_lambda_k2, ev_subln, ev_w_out,
           od_w_in, od_w_out):
    depth, d = norm_pre.shape
    mod = _adaln_mod(c, ada_w, ada_b)
    rope = _rope_tables(positions)
    half = N_UNITS // 2
    even_dst = ([(j % half, j // half) for j in range(N_KINDS * half)]
                + [(half + j % half, j // half) for j in range(N_KINDS * half)])
    odd_dst = [(j % N_UNITS, j // N_UNITS) for j in range(N_SLABS)]
    even_rope = set(range(4 * half, 6 * half))
    odd_rope = set(range(2 * N_UNITS))
    ev_w_main, ev_wf = _even_weight_layout(ev_w_in)
    od_w_main = (od_w_in.astype(jnp.bfloat16),)
    w_outs = (ev_w_out.astype(jnp.bfloat16), od_w_out.astype(jnp.bfloat16))
    npre = norm_pre.reshape(depth, 1, d)
    npost = norm_post.reshape(depth, 1, d)
    b_forget = jnp.pad(ev_b_forget, ((0, 0), (0, LANES - N_HEADS_FOX))).reshape(-1, 1, LANES)
    lam_params = (ev_lambda_q1, ev_lambda_k1, ev_lambda_q2, ev_lambda_k2, ev_subln)
    pending = None
    for layer in range(depth):
        idx = layer // 2
        if layer % 2 == 0:
            proj, fa, x = _in_projection(x, mod, layer, npre, ev_w_main, idx, ev_wf, rope, even_rope, even_dst,
                                         pending)
            aug = _forget_prefix(fa, b_forget, idx)
            lam_init = 0.8 - 0.6 * math.exp(-0.3 * layer)
            o = _attention(proj, "even", aug=aug, params=lam_params, par_idx=idx, lam_init=lam_init)
        else:
            proj, _, x = _in_projection(x, mod, layer, npre, od_w_main, idx, None, rope, odd_rope, odd_dst,
                                        pending)
            o = _attention(proj, "odd")
        pending = (o, w_outs[layer % 2], idx, npost, layer)
    o, w_out, idx, _, layer = pending
    return _out_projection(o, w_out, idx, x, mod, layer, npost)
```

```python
import functools
import math

import jax
import jax.numpy as jnp
from jax import lax
from jax.experimental import pallas as pl
from jax.experimental.pallas import tpu as pltpu

D_MODEL = 1024
HEAD_DIM = 64
N_HEADS_FOX = 8
ROT_DIM = HEAD_DIM // 4
ROPE_THETA = 500000.0
NORM_EPS = 1e-6
DILATED_CONFIGS = ((128, 1), (512, 4), (2048, 16))

LANES = 128
MXU_COLS = 256
N_SLABS = 32
N_UNITS = 8
N_KINDS = 4
UNITS_PER_STEP = 1
TQ = 512
TK = 512
TM = 512
NEG = -1e30
LOG2E = math.log2(math.e)
VMEM_LIMIT = 48 * 1024 * 1024


def _cparams(n_axes):
    return pltpu.CompilerParams(
        dimension_semantics=("arbitrary",) * n_axes, vmem_limit_bytes=VMEM_LIMIT)


def _mod_kernel(c_ref, w_ref, b_ref, o_ref):
    c = c_ref[...]
    cond = c * jax.nn.sigmoid(c)
    for j in range(3):
        o_ref[0, j] = (jnp.dot(cond, w_ref[0, :, j * D_MODEL:(j + 1) * D_MODEL],
                               preferred_element_type=jnp.float32) + b_ref[0, j])


def _adaln_mod(c, ada_w, ada_b):
    depth = ada_w.shape[0]
    b = c.shape[0]
    out = pl.pallas_call(
        _mod_kernel,
        grid=(depth,),
        in_specs=[
            pl.BlockSpec((b, D_MODEL), lambda l: (0, 0)),
            pl.BlockSpec((1, D_MODEL, 3 * D_MODEL), lambda l: (l, 0, 0)),
            pl.BlockSpec((1, 3, 1, D_MODEL), lambda l: (l, 0, 0, 0)),
        ],
        out_specs=pl.BlockSpec((1, 3, b, D_MODEL), lambda l: (l, 0, 0, 0)),
        out_shape=jax.ShapeDtypeStruct((depth, 3, b, D_MODEL), jnp.float32),
        compiler_params=_cparams(1),
        name="adaln_mod",
    )(c, ada_w, ada_b.reshape(depth, 3, 1, D_MODEL))
    return out.reshape(depth, 3, b, 1, D_MODEL)


ROPE_PACK = LANES // (ROT_DIM // 2)


def _rope_kernel(pos_ref, c_ref, s1_ref, s2_ref):
    nf = ROT_DIM // 2
    pos = pos_ref[0].astype(jnp.float32)
    lane = lax.broadcasted_iota(jnp.int32, (1, LANES), 1)
    fidx = (lane % nf).astype(jnp.float32)
    theta = jnp.full((1, LANES), ROPE_THETA, jnp.float32)
    inv_freq = jnp.exp(-(fidx * (2.0 / ROT_DIM)) * jnp.log(theta))
    ang = pos * inv_freq
    n_rows = pos.shape[0]

    def split3(v):
        hi = v.astype(jnp.bfloat16)
        r1 = v - hi.astype(jnp.float32)
        mid = r1.astype(jnp.bfloat16)
        lo = (r1 - mid.astype(jnp.float32)).astype(jnp.bfloat16)
        return jnp.concatenate([hi, mid, lo], axis=0)

    cos3 = split3(jnp.cos(ang))
    sin3 = split3(jnp.sin(ang))
    src = lax.broadcasted_iota(jnp.int32, (LANES, LANES), 0)
    dst = lax.broadcasted_iota(jnp.int32, (LANES, LANES), 1)
    cc = dst % HEAD_DIM
    for j in range(ROPE_PACK):
        hit = (src // nf == j) & (src % nf == cc % nf)
        e_c = jnp.where(hit & (cc < ROT_DIM), 1.0, 0.0).astype(jnp.bfloat16)
        e_1 = jnp.where(hit & (cc < nf), -1.0, 0.0).astype(jnp.bfloat16)
        e_2 = jnp.where(hit & (cc >= nf) & (cc < ROT_DIM), 1.0, 0.0).astype(jnp.bfloat16)
        rows = pl.ds(j, n_rows, stride=ROPE_PACK)
        for ref, tab3, e, base in ((c_ref, cos3, e_c, 1.0), (s1_ref, sin3, e_1, 0.0), (s2_ref, sin3, e_2, 0.0)):
            t3 = jnp.dot(tab3, e, preferred_element_type=jnp.float32)
            t = (t3[0:n_rows] + t3[n_rows:2 * n_rows]) + t3[2 * n_rows:]
            if base:
                t = t + jnp.where(dst[0:1] % HEAD_DIM < ROT_DIM, 0.0, base)
            ref[0, rows, :] = t


def _rope_tables(positions):
    b, s = positions.shape
    packed = jnp.repeat(positions.reshape(b, s // ROPE_PACK, ROPE_PACK), ROT_DIM // 2, axis=-1)
    spec = pl.BlockSpec((1, s, LANES), lambda i: (i, 0, 0))
    shp = jax.ShapeDtypeStruct((b, s, LANES), jnp.float32)
    return pl.pallas_call(
        _rope_kernel,
        grid=(b,),
        in_specs=[pl.BlockSpec((1, s // ROPE_PACK, LANES), lambda i: (i, 0, 0))],
        out_specs=[spec, spec, spec],
        out_shape=[shp, shp, shp],
        compiler_params=_cparams(1),
        name="rope_tables",
    )(packed)


def _residual_update(o_ref, wout_ref, rows, x, gate, npost):
    cat = jnp.concatenate([o_ref[0, j, rows, :] for j in range(N_UNITS)], axis=1)
    y = jnp.dot(cat, wout_ref[...], preferred_element_type=jnp.float32)
    var = jnp.mean(y * y, axis=-1, keepdims=True)
    return x + gate * (y * lax.rsqrt(var + NORM_EPS) * npost)


SUB_TILES = 2


def _inproj_kernel(*refs, rope_slabs, slab_dst, n_w, skip_cols, with_forget, with_residual):
    refs = list(refs)
    if with_residual:
        o_ref, wout_ref, gate_ref, npost_ref = refs[:4]
        refs = refs[4:]
    x_ref, shift_ref, scale_ref, npre_ref = refs[:4]
    w_refs, refs = refs[4:4 + n_w], refs[4 + n_w:]
    wf_ref = refs.pop(0) if with_forget else None
    c_ref, s1_ref, s2_ref = refs[:3]
    refs = refs[3:]
    xout_ref = refs.pop(0) if with_residual else None
    out_ref = refs.pop(0)
    fa_ref = refs.pop(0) if with_forget else None

    tm = x_ref.shape[1]
    sub_rows = [slice(t * tm // SUB_TILES, (t + 1) * tm // SUB_TILES) for t in range(SUB_TILES)]
    hbs = []
    for rows in sub_rows:
        x = x_ref[0, rows, :]
        if with_residual:
            x = _residual_update(o_ref, wout_ref, rows, x, gate_ref[...], npost_ref[...])
            xout_ref[0, rows, :] = x
        var = jnp.mean(x * x, axis=-1, keepdims=True)
        h = (x * lax.rsqrt(var + NORM_EPS) * npre_ref[...]) * (1.0 + scale_ref[...]) + shift_ref[...]
        hbs.append(h.astype(jnp.bfloat16))
    if skip_cols is not None:
        c0, c1 = skip_cols
        w_full, w_tail = w_refs[0], refs.pop()

        @pl.when((pl.program_id(0) == 0) & (pl.program_id(1) == 0))
        def _():
            w_tail[...] = w_full[:, c1:]

        chunks = ([(w_full, k) for k in range(c0 // MXU_COLS)]
                  + [(w_tail, k) for k in range(w_tail.shape[1] // MXU_COLS)])
    else:
        chunks = [(w_ref, k) for w_ref in w_refs for k in range(w_ref.shape[1] // MXU_COLS)]
    assert len(chunks) == N_SLABS * LANES // MXU_COLS
    for rows, hb in zip(sub_rows, hbs):
        cos = c_ref[0, rows, :]
        s1 = s1_ref[0, rows, :]
        s2 = s2_ref[0, rows, :]
        for c, (w_ref, k) in enumerate(chunks):
            r = jnp.dot(hb, w_ref[:, k * MXU_COLS:(k + 1) * MXU_COLS],
                        preferred_element_type=jnp.float32)
            for half in range(MXU_COLS // LANES):
                j = c * (MXU_COLS // LANES) + half
                y = r[:, half * LANES:(half + 1) * LANES]
                if j in rope_slabs:
                    y = (y * cos + pltpu.roll(y, LANES - ROT_DIM // 2, 1) * s1
                         + pltpu.roll(y, ROT_DIM // 2, 1) * s2)
                out_ref[(0,) + slab_dst[j] + (rows, slice(None))] = y.astype(jnp.bfloat16)
        if with_forget:
            fa_ref[0, rows, :] = jnp.dot(hb, wf_ref[...], preferred_element_type=jnp.float32)


def _mod_spec(layer, which):
    return pl.BlockSpec((None, None, None, 1, D_MODEL), lambda bi, i: (layer, which, bi, 0, 0))


def _layer_spec(shape, index):
    return pl.BlockSpec((None,) + tuple(shape[1:]), lambda bi, i: (index,) + (0,) * (len(shape) - 1))


def _in_projection(x, mod, layer, norm_pre, w, w_idx, wf, rope, rope_slabs, slab_dst, residual=None,
                   skip_cols=None):
    b, s, d = x.shape
    with_forget = wf is not None
    with_residual = residual is not None
    row = lambda bi, i: (bi, i, 0)
    in_specs, args = [], []
    if with_residual:
        o, w_out, out_idx, norm_post, prev = residual
        in_specs += [pl.BlockSpec((1, N_UNITS, TM, LANES), lambda bi, i: (bi, 0, i, 0)),
                     _layer_spec(w_out.shape, out_idx), _mod_spec(prev, 2), _layer_spec(norm_post.shape, prev)]
        args += [o, w_out, mod, norm_post]
    in_specs += [pl.BlockSpec((1, TM, d), row), _mod_spec(layer, 0), _mod_spec(layer, 1),
                 _layer_spec(norm_pre.shape, layer)] + [_layer_spec(part.shape, w_idx) for part in w]
    args += [x, mod, mod, norm_pre] + list(w)
    if with_forget:
        in_specs.append(_layer_spec(wf.shape, w_idx))
        args.append(wf)
    in_specs += [pl.BlockSpec((1, TM, LANES), row)] * 3
    args += list(rope)
    out_specs, out_shape = [], []
    if with_residual:
        out_specs.append(pl.BlockSpec((1, TM, d), row))
        out_shape.append(jax.ShapeDtypeStruct((b, s, d), jnp.float32))
    out_specs.append(pl.BlockSpec((1, N_UNITS, N_KINDS, TM, LANES), lambda bi, i: (bi, 0, 0, i, 0)))
    out_shape.append(jax.ShapeDtypeStruct((b, N_UNITS, N_KINDS, s, LANES), jnp.bfloat16))
    if with_forget:
        out_specs.append(pl.BlockSpec((1, TM, LANES), row))
        out_shape.append(jax.ShapeDtypeStruct((b, s, LANES), jnp.float32))
    res = list(pl.pallas_call(
        functools.partial(_inproj_kernel, rope_slabs=frozenset(rope_slabs), slab_dst=tuple(slab_dst), n_w=len(w),
                          skip_cols=skip_cols,
                          with_forget=with_forget, with_residual=with_residual),
        grid=(b, s // TM),
        in_specs=in_specs,
        out_specs=out_specs,
        out_shape=out_shape,
        scratch_shapes=([] if skip_cols is None else
                        [pltpu.VMEM((d, w[0].shape[-1] - skip_cols[1]), jnp.bfloat16)]),
        compiler_params=_cparams(2),
        name="in_projection",
    )(*args))
    x_new = res.pop(0) if with_residual else x
    proj = res.pop(0)
    fa = res.pop(0) if with_forget else None
    return proj, fa, x_new


F_BLOCK = 256


def _fgate_kernel(fa_ref, b_ref, g_ref):
    s = fa_ref.shape[1]
    nh = N_HEADS_FOX
    z = fa_ref[0] + b_ref[...]
    logf = jnp.minimum(z, 0.0) - jnp.log(1.0 + jnp.exp(-jnp.abs(z)))
    r_i = lax.broadcasted_iota(jnp.int32, (F_BLOCK, F_BLOCK), 0)
    c_i = lax.broadcasted_iota(jnp.int32, (F_BLOCK, F_BLOCK), 1)
    tri = jnp.where(c_i <= r_i, 1.0, 0.0).astype(jnp.bfloat16)
    lane = lax.broadcasted_iota(jnp.int32, (1, LANES), 1)

    def split3(v):
        hi = v.astype(jnp.bfloat16)
        r1 = v - hi.astype(jnp.float32)
        mid = r1.astype(jnp.bfloat16)
        lo = (r1 - mid.astype(jnp.float32)).astype(jnp.bfloat16)
        return hi, mid, lo

    carry = jnp.zeros((1, LANES), jnp.float32)
    for i in range(s // F_BLOCK):
        rows = slice(i * F_BLOCK, (i + 1) * F_BLOCK)
        parts = jnp.concatenate(split3(logf[rows]), axis=1)
        cs3 = jnp.dot(tri, parts, preferred_element_type=jnp.float32)
        f = (cs3[:, :LANES] + cs3[:, LANES:2 * LANES]) + cs3[:, 2 * LANES:] + carry
        carry = f[F_BLOCK - 1:F_BLOCK, :]
        hi, mid, lo = split3(f * LOG2E)
        packed = (jnp.where(lane < nh, hi.astype(jnp.float32), 0.0)
                  + pltpu.roll(jnp.where(lane < nh, mid.astype(jnp.float32), 0.0), nh, 1)
                  + pltpu.roll(jnp.where(lane < nh, lo.astype(jnp.float32), 0.0), 2 * nh, 1))
        ones_hi = jnp.where((lane >= 3 * nh) & (lane < 6 * nh), 1.0, 0.0)
        ones_lo = jnp.where(lane < 3 * nh, 1.0, 0.0)
        g_ref[0, 0, 0, rows, :] = (ones_hi - packed).astype(jnp.bfloat16)
        g_ref[0, 0, 1, rows, :] = (ones_lo + pltpu.roll(packed, 3 * nh, 1)).astype(jnp.bfloat16)
    g_ref[0, 1] = jnp.zeros((2, s, LANES), jnp.bfloat16)


def _forget_prefix(fa, b_forget, idx):
    b, s, _ = fa.shape
    return pl.pallas_call(
        _fgate_kernel,
        grid=(b,),
        in_specs=[pl.BlockSpec((1, s, LANES), lambda i: (i, 0, 0)),
                  pl.BlockSpec((None, 1, LANES), lambda i: (idx, 0, 0))],
        out_specs=pl.BlockSpec((1, 2, 2, s, LANES), lambda i: (i, 0, 0, 0, 0)),
        out_shape=jax.ShapeDtypeStruct((b, 2, 2, s, LANES), jnp.bfloat16),
        compiler_params=_cparams(1),
        name="forget_prefix",
    )(fa, b_forget)


V_ROWS = LANES + 16


KPQ = TQ // TK


def _n_bias_tables(mode, s_len):
    if mode == "even":
        return KPQ
    bounded = max((w // d) * d for (w, d) in DILATED_CONFIGS if (w // d) * d < s_len - 1)
    dist = 0
    while dist * TK - (TK - 1) <= bounded:
        dist += 1
    return dist + KPQ


LOG2_3 = math.log2(3.0)


def _fill_bias_tables(tab_ref, mode, s_len):
    n_tab = tab_ref.shape[0]
    kr = lax.broadcasted_iota(jnp.int32, (TK, TQ), 0)
    qc = lax.broadcasted_iota(jnp.int32, (TK, TQ), 1)
    for idx in range(n_tab):
        delta = (idx - (KPQ - 1)) * TK + qc - kr
        if mode == "even":
            bias = jnp.where(delta >= 0, 0.0, NEG)
        else:
            mult = jnp.zeros((TK, TQ), jnp.int32)
            for (w, d) in DILATED_CONFIGS:
                reach = (w // d) * d
                if idx == n_tab - 1 and reach < s_len - 1:
                    continue
                ok = (delta >= 0) & (jnp.bitwise_and(delta, d - 1) == 0) & (delta <= reach)
                mult = mult + ok.astype(jnp.int32)
            bias = jnp.where(mult == 0, NEG, jnp.where(mult == 1, 0.0, jnp.where(mult == 2, 1.0, LOG2_3)))
        tab_ref[idx] = bias


def _attn_kernel(*refs, mode, lam_init, par_idx):
    if mode == "even":
        (p_ref, aug_ref, lq1_ref, lk1_ref, lq2_ref, lk2_ref, subln_ref, o_ref,
         vt, sb00, sb01, sb10, sb11, cm_buf, m1, a1, m2, a2, tab_ref) = refs
    else:
        (p_ref, o_ref, vt, sb00, sb01, sb10, sb11, cm_buf, m1, a1, m2, a2, tab_ref) = refs
    s_len = p_ref.shape[3]

    @pl.when((pl.program_id(0) == 0) & (pl.program_id(1) == 0))
    def _():
        _fill_bias_tables(tab_ref, mode, s_len)

    def unit(uu, carry):
        u = pl.program_id(1) * UNITS_PER_STEP + uu
        lane = lax.broadcasted_iota(jnp.int32, (1, LANES), 1)
        lo = lane < HEAD_DIM
        nh = N_HEADS_FOX
        nt = (((1,), (1,)), ((), ()))

        vtt = p_ref[0, uu, 2].astype(jnp.float32).T
        ones_rows = jnp.where(lax.broadcasted_iota(jnp.int32, (V_ROWS - LANES, TK), 0) == 0, 1.0, 0.0)
        for j in range(s_len // TK):
            vt[j, 0:LANES, :] = vtt[:, j * TK:(j + 1) * TK].astype(jnp.bfloat16)
            vt[j, LANES:V_ROWS, :] = ones_rows.astype(jnp.bfloat16)

        if mode == "even":
            is_fox = u < N_UNITS // 2

        state = ((m1, a1), (m2, a2))
        n_tab = tab_ref.shape[0]

        def make_lhs(qi):
            q0 = qi * TQ
            qs = p_ref[0, uu, 0, q0:q0 + TQ, :].astype(jnp.float32) * (HEAD_DIM ** -0.5 * LOG2E)
            lhs = [jnp.where(lo, qs, 0.0).astype(jnp.bfloat16),
                   jnp.where(lo, 0.0, qs).astype(jnp.bfloat16)]
            if mode == "even":
                gq = aug_ref[0, 0, 1, q0:q0 + TQ, :].astype(jnp.float32)
                for i in range(2):
                    h = 2 * u + i
                    sel = (lane % nh == h) & (lane < 6 * nh)
                    qa = jnp.where(sel, gq, 0.0)
                    lhs[i] = jnp.concatenate([lhs[i], qa.astype(jnp.bfloat16)], axis=1)
            return lhs

        half = TK // 2
        assert TQ == TK
        s_bufs = ((sb00, sb01), (sb10, sb11))
        z0 = pl.multiple_of(jnp.minimum(pl.program_id(0), 0), TK)

        def scores_into(slot, lhs, qi, kt):
            k0 = kt * TK

            def keys(r0, n):
                kb = p_ref[0, uu, 1, r0:r0 + n, :]
                if mode == "even":
                    kb = jnp.concatenate([kb, aug_ref[0, 0, 0, r0:r0 + n, :]], axis=1)
                return kb

            dist = qi - kt
            for i in range(2):
                if dist > 0:
                    st = lax.dot_general(keys(k0, TK), lhs[i], nt,
                                         preferred_element_type=jnp.float32)
                    if mode != "even":
                        st = st + tab_ref[min(dist, n_tab - 1)]
                    s_bufs[slot][i][pl.ds(z0, TK), :] = st
                    cm_buf[slot, i] = jnp.max(st, axis=0, keepdims=True)
                else:
                    sa = lax.dot_general(keys(k0, half), lhs[i], nt,
                                         preferred_element_type=jnp.float32) + tab_ref[0, 0:half, :]
                    sb = lax.dot_general(keys(k0 + half, half), lhs[i][half:, :], nt,
                                         preferred_element_type=jnp.float32) + tab_ref[0, half:, half:]
                    s_bufs[slot][i][pl.ds(z0, half), :] = sa
                    s_bufs[slot][i][pl.ds(z0 + half, half), half:] = sb
                    ca = jnp.max(sa, axis=0, keepdims=True)
                    cb = jnp.max(sb, axis=0, keepdims=True)
                    cm_buf[slot, i] = jnp.concatenate(
                        [ca[:, :half], jnp.maximum(ca[:, half:], cb)], axis=1)

        def consume(slot, qi, kt):
            for i in range(2):
                m_r, a_r = state[i]
                m_prev = m_r[...]
                m_next = jnp.maximum(m_prev, cm_buf[slot, i])
                alpha = jnp.exp2(m_prev - m_next)
                if qi > kt:
                    pt = jnp.exp2(s_bufs[slot][i][pl.ds(z0, TK), :] - m_next).astype(jnp.bfloat16)
                    a_r[...] = alpha * a_r[...] + jnp.dot(vt[kt], pt, preferred_element_type=jnp.float32)
                else:
                    pa = jnp.exp2(s_bufs[slot][i][pl.ds(z0, half), :] - m_next).astype(jnp.bfloat16)
                    pb = jnp.exp2(s_bufs[slot][i][pl.ds(z0 + half, half), half:]
                                  - m_next[:, half:]).astype(jnp.bfloat16)
                    pva = jnp.dot(vt[kt, :, 0:half], pa, preferred_element_type=jnp.float32)
                    pvb = jnp.dot(vt[kt, :, half:], pb, preferred_element_type=jnp.float32)
                    a_r[:, 0:half] = alpha[:, :half] * a_r[:, 0:half] + pva[:, :half]
                    a_r[:, half:] = alpha[:, half:] * a_r[:, half:] + (pva[:, half:] + pvb)
                m_r[...] = m_next

        def finish(qi):
            q0 = qi * TQ
            outs = []
            for (m_r, a_r) in state:
                acc = a_r[...]
                inv_l = 1.0 / acc[LANES:LANES + 1, :]
                outs.append((acc[0:LANES, :] * inv_l).T)
            o1, o2 = outs
            g = p_ref[0, uu, 3, q0:q0 + TQ, :].astype(jnp.float32)
            gate = g * jax.nn.sigmoid(g)
            out = jnp.where(lo, o1, o2)
            if mode == "even":
                pr = slice(par_idx, par_idx + 1)
                lam = (jnp.exp(jnp.sum(lq1_ref[pr, :] * lk1_ref[pr, :], axis=1, keepdims=True))
                       - jnp.exp(jnp.sum(lq2_ref[pr, :] * lk2_ref[pr, :], axis=1, keepdims=True)) + lam_init)
                d = o1 - lam * o2
                var = jnp.mean(d * d, axis=-1, keepdims=True)
                dn = (d * lax.rsqrt(var + NORM_EPS) * subln_ref[pr, :]) * (1.0 - lam_init)
                out = jnp.where(is_fox, out, dn)
            o_ref[0, uu, q0:q0 + TQ, :] = (out * gate).astype(jnp.bfloat16)

        pairs = [(qi, kt) for qi in range(s_len // TQ) for kt in range(qi + 1)]
        lhs = make_lhs(0)
        scores_into(0, lhs, 0, 0)
        for n, (qi, kt) in enumerate(pairs):
            if kt == 0:
                for (m_r, a_r) in state:
                    m_r[...] = jnp.full((1, TQ), NEG, jnp.float32)
                    a_r[...] = jnp.zeros((V_ROWS, TQ), jnp.float32)
            if n + 1 < len(pairs):
                nqi, nkt = pairs[n + 1]
                if nkt == 0:
                    lhs = make_lhs(nqi)
                scores_into((n + 1) % 2, lhs, nqi, nkt)
            consume(n % 2, qi, kt)
            if kt == qi:
                finish(qi)
        return carry

    lax.fori_loop(0, UNITS_PER_STEP, unit, 0)


def _attention(proj, mode, aug=None, params=None, par_idx=0, lam_init=0.0):
    b, _, _, s, _ = proj.shape
    in_specs = [pl.BlockSpec((1, UNITS_PER_STEP, N_KINDS, s, LANES), lambda bi, p: (bi, p, 0, 0, 0))]
    args = [proj]
    scratch = []
    if mode == "even":
        in_specs.append(pl.BlockSpec((1, 1, 2, s, LANES),
                                     lambda bi, p: (bi, p * UNITS_PER_STEP // (N_UNITS // 2), 0, 0, 0)))
        args.append(aug)
    if mode == "even":
        for arr in params:
            in_specs.append(pl.BlockSpec(arr.shape, lambda bi, p: (0, 0)))
            args.append(arr)
    scratch.append(pltpu.VMEM((s // TK, V_ROWS, TK), jnp.bfloat16))
    scratch += [pltpu.VMEM((TK, TQ), jnp.float32)] * 4
    scratch.append(pltpu.VMEM((2, 2, 1, TQ), jnp.float32))
    scratch += [pltpu.VMEM((1, TQ), jnp.float32), pltpu.VMEM((V_ROWS, TQ), jnp.float32)] * 2
    scratch.append(pltpu.VMEM((_n_bias_tables(mode, s), TK, TQ), jnp.float32))
    return pl.pallas_call(
        functools.partial(_attn_kernel, mode=mode, lam_init=lam_init, par_idx=par_idx),
        grid=(b, N_UNITS // UNITS_PER_STEP),
        in_specs=in_specs,
        out_specs=pl.BlockSpec((1, UNITS_PER_STEP, s, LANES), lambda bi, p: (bi, p, 0, 0)),
        out_shape=jax.ShapeDtypeStruct((b, N_UNITS, s, LANES), jnp.bfloat16),
        scratch_shapes=scratch,
        compiler_params=_cparams(2),
        name="attention_" + mode,
    )(*args)


def _outproj_kernel(o_ref, w_ref, x_ref, gate_ref, npost_ref, out_ref):
    out_ref[0] = _residual_update(o_ref, w_ref, slice(None), x_ref[0], gate_ref[...], npost_ref[...])


def _out_projection(o, w_out, out_idx, x, mod, layer, norm_post):
    b, s, d = x.shape
    return pl.pallas_call(
        _outproj_kernel,
        grid=(b, s // TM),
        in_specs=[
            pl.BlockSpec((1, N_UNITS, TM, LANES), lambda bi, i: (bi, 0, i, 0)),
            _layer_spec(w_out.shape, out_idx),
            pl.BlockSpec((1, TM, d), lambda bi, i: (bi, i, 0)),
            _mod_spec(layer, 2),
            _layer_spec(norm_post.shape, layer),
        ],
        out_specs=pl.BlockSpec((1, TM, d), lambda bi, i: (bi, i, 0)),
        out_shape=jax.ShapeDtypeStruct((b, s, d), jnp.float32),
        compiler_params=_cparams(2),
        name="out_projection",
    )(o, w_out, x, mod, norm_post)


def _even_weight_layout(w_in):
    f0 = 3 * N_HEADS_FOX * HEAD_DIM
    f1 = f0 + N_HEADS_FOX
    wf = jnp.pad(w_in[..., f0:f1], ((0, 0), (0, 0), (0, LANES - N_HEADS_FOX))).astype(jnp.bfloat16)
    return (w_in.astype(jnp.bfloat16),), wf


def kernel(x, c, positions, norm_pre, norm_post, ada_w, ada_b, ev_w_in, ev_b_forget,
           ev_lambda_q1, ev_lambda_k1, ev_lambda_q2, ev_lambda_k2, ev_subln, ev_w_out,
           od_w_in, od_w_out):
    depth, d = norm_pre.shape
    mod = _adaln_mod(c, ada_w, ada_b)
    rope = _rope_tables(positions)
    half = N_UNITS // 2
    even_dst = ([(j % half, j // half) for j in range(N_KINDS * half)]
                + [(half + j % half, j // half) for j in range(N_KINDS * half)])
    odd_dst = [(j % N_UNITS, j // N_UNITS) for j in range(N_SLABS)]
    even_rope = set(range(4 * half, 6 * half))
    odd_rope = set(range(2 * N_UNITS))
    ev_w_main, ev_wf = _even_weight_layout(ev_w_in)
    od_w_main = (od_w_in.astype(jnp.bfloat16),)
    w_outs = (ev_w_out.astype(jnp.bfloat16), od_w_out.astype(jnp.bfloat16))
    npre = norm_pre.reshape(depth, 1, d)
    npost = norm_post.reshape(depth, 1, d)
    b_forget = jnp.pad(ev_b_forget, ((0, 0), (0, LANES - N_HEADS_FOX))).reshape(-1, 1, LANES)
    f0 = 3 * N_HEADS_FOX * HEAD_DIM
    lam_params = (ev_lambda_q1, ev_lambda_k1, ev_lambda_q2, ev_lambda_k2, ev_subln)
    pending = None
    for layer in range(depth):
        idx = layer // 2
        if layer % 2 == 0:
            proj, fa, x = _in_projection(x, mod, layer, npre, ev_w_main, idx, ev_wf, rope, even_rope, even_dst,
                                         pending, skip_cols=(f0, f0 + N_HEADS_FOX))
            aug = _forget_prefix(fa, b_forget, idx)
            lam_init = 0.8 - 0.6 * math.exp(-0.3 * layer)
            o = _attention(proj, "even", aug=aug, params=lam_params, par_idx=idx, lam_init=lam_init)
        else:
            proj, _, x = _in_projection(x, mod, layer, npre, od_w_main, idx, None, rope, odd_rope, odd_dst,
                                        pending)
            o = _attention(proj, "odd")
        pending = (o, w_outs[layer % 2], idx, npost, layer)
    o, w_out, idx, _, layer = pending
    return _out_projection(o, w_out, idx, x, mod, layer, npost)
```

```python
import functools
import math

import jax
import jax.numpy as jnp
from jax import lax
from jax.experimental import pallas as pl
from jax.experimental.pallas import tpu as pltpu

D_MODEL = 1024
HEAD_DIM = 64
N_HEADS_FOX = 8
ROT_DIM = HEAD_DIM // 4
ROPE_THETA = 500000.0
NORM_EPS = 1e-6
DILATED_CONFIGS = ((128, 1), (512, 4), (2048, 16))

LANES = 128
MXU_COLS = 256
N_SLABS = 32
N_UNITS = 8
N_KINDS = 4
UNITS_PER_STEP = 1
TQ = 512
TK = 512
TM = 512
NEG = -1e30
LOG2E = math.log2(math.e)
VMEM_LIMIT = 48 * 1024 * 1024


def _cparams(n_axes):
    return pltpu.CompilerParams(
        dimension_semantics=("arbitrary",) * n_axes, vmem_limit_bytes=VMEM_LIMIT)


def _mod_kernel(c_ref, w_ref, b_ref, o_ref):
    c = c_ref[...]
    cond = c * jax.nn.sigmoid(c)
    for j in range(3):
        o_ref[0, j] = (jnp.dot(cond, w_ref[0, :, j * D_MODEL:(j + 1) * D_MODEL],
                               preferred_element_type=jnp.float32) + b_ref[0, j])


def _adaln_mod(c, ada_w, ada_b):
    depth = ada_w.shape[0]
    b = c.shape[0]
    out = pl.pallas_call(
        _mod_kernel,
        grid=(depth,),
        in_specs=[
            pl.BlockSpec((b, D_MODEL), lambda l: (0, 0)),
            pl.BlockSpec((1, D_MODEL, 3 * D_MODEL), lambda l: (l, 0, 0)),
            pl.BlockSpec((1, 3, 1, D_MODEL), lambda l: (l, 0, 0, 0)),
        ],
        out_specs=pl.BlockSpec((1, 3, b, D_MODEL), lambda l: (l, 0, 0, 0)),
        out_shape=jax.ShapeDtypeStruct((depth, 3, b, D_MODEL), jnp.float32),
        compiler_params=_cparams(1),
        name="adaln_mod",
    )(c, ada_w, ada_b.reshape(depth, 3, 1, D_MODEL))
    return out.reshape(depth, 3, b, 1, D_MODEL)


ROPE_PACK = LANES // (ROT_DIM // 2)


def _rope_kernel(pos_ref, c_ref, s1_ref, s2_ref):
    nf = ROT_DIM // 2
    pos = pos_ref[0].astype(jnp.float32)
    lane = lax.broadcasted_iota(jnp.int32, (1, LANES), 1)
    fidx = (lane % nf).astype(jnp.float32)
    theta = jnp.full((1, LANES), ROPE_THETA, jnp.float32)
    inv_freq = jnp.exp(-(fidx * (2.0 / ROT_DIM)) * jnp.log(theta))
    ang = pos * inv_freq
    n_rows = pos.shape[0]

    def split3(v):
        hi = v.astype(jnp.bfloat16)
        r1 = v - hi.astype(jnp.float32)
        mid = r1.astype(jnp.bfloat16)
        lo = (r1 - mid.astype(jnp.float32)).astype(jnp.bfloat16)
        return jnp.concatenate([hi, mid, lo], axis=0)

    cos3 = split3(jnp.cos(ang))
    sin3 = split3(jnp.sin(ang))
    src = lax.broadcasted_iota(jnp.int32, (LANES, LANES), 0)
    dst = lax.broadcasted_iota(jnp.int32, (LANES, LANES), 1)
    cc = dst % HEAD_DIM
    for j in range(ROPE_PACK):
        hit = (src // nf == j) & (src % nf == cc % nf)
        e_c = jnp.where(hit & (cc < ROT_DIM), 1.0, 0.0).astype(jnp.bfloat16)
        e_1 = jnp.where(hit & (cc < nf), -1.0, 0.0).astype(jnp.bfloat16)
        e_2 = jnp.where(hit & (cc >= nf) & (cc < ROT_DIM), 1.0, 0.0).astype(jnp.bfloat16)
        rows = pl.ds(j, n_rows, stride=ROPE_PACK)
        for ref, tab3, e, base in ((c_ref, cos3, e_c, 1.0), (s1_ref, sin3, e_1, 0.0), (s2_ref, sin3, e_2, 0.0)):
            t3 = jnp.dot(tab3, e, preferred_element_type=jnp.float32)
            t = (t3[0:n_rows] + t3[n_rows:2 * n_rows]) + t3[2 * n_rows:]
            if base:
                t = t + jnp.where(dst[0:1] % HEAD_DIM < ROT_DIM, 0.0, base)
            ref[0, rows, :] = t


def _rope_tables(positions):
    b, s = positions.shape
    packed = jnp.repeat(positions.reshape(b, s // ROPE_PACK, ROPE_PACK), ROT_DIM // 2, axis=-1)
    spec = pl.BlockSpec((1, s, LANES), lambda i: (i, 0, 0))
    shp = jax.ShapeDtypeStruct((b, s, LANES), jnp.float32)
    return pl.pallas_call(
        _rope_kernel,
        grid=(b,),
        in_specs=[pl.BlockSpec((1, s // ROPE_PACK, LANES), lambda i: (i, 0, 0))],
        out_specs=[spec, spec, spec],
        out_shape=[shp, shp, shp],
        compiler_params=_cparams(1),
        name="rope_tables",
    )(packed)


def _residual_update(o_refs, wout_ref, rows, x, gate, npost):
    cat = jnp.concatenate([o_ref[0, j, rows, :] for o_ref in o_refs for j in range(o_ref.shape[1])],
                          axis=1)
    y = jnp.dot(cat, wout_ref[...], preferred_element_type=jnp.float32)
    var = jnp.mean(y * y, axis=-1, keepdims=True)
    return x + gate * (y * lax.rsqrt(var + NORM_EPS) * npost)


SUB_TILES = 2


def _inproj_kernel(*refs, rope_slabs, slab_dst, n_w, n_o, with_forget, with_residual):
    refs = list(refs)
    if with_residual:
        o_refs, refs = refs[:n_o], refs[n_o:]
        wout_ref, gate_ref, npost_ref = refs[:3]
        refs = refs[3:]
    x_ref, shift_ref, scale_ref, npre_ref = refs[:4]
    w_refs, refs = refs[4:4 + n_w], refs[4 + n_w:]
    wf_ref = refs.pop(0) if with_forget else None
    c_ref, s1_ref, s2_ref = refs[:3]
    refs = refs[3:]
    xout_ref = refs.pop(0) if with_residual else None
    out_ref = refs.pop(0)
    fa_ref = refs.pop(0) if with_forget else None

    tm = x_ref.shape[1]
    sub_rows = [slice(t * tm // SUB_TILES, (t + 1) * tm // SUB_TILES) for t in range(SUB_TILES)]
    hbs = []
    for rows in sub_rows:
        x = x_ref[0, rows, :]
        if with_residual:
            x = _residual_update(o_refs, wout_ref, rows, x, gate_ref[...], npost_ref[...])
            xout_ref[0, rows, :] = x
        var = jnp.mean(x * x, axis=-1, keepdims=True)
        h = (x * lax.rsqrt(var + NORM_EPS) * npre_ref[...]) * (1.0 + scale_ref[...]) + shift_ref[...]
        hbs.append(h.astype(jnp.bfloat16))
    chunks = [(w_ref, k) for w_ref in w_refs for k in range(w_ref.shape[1] // MXU_COLS)]
    assert len(chunks) == N_SLABS * LANES // MXU_COLS
    for rows, hb in zip(sub_rows, hbs):
        cos = c_ref[0, rows, :]
        s1 = s1_ref[0, rows, :]
        s2 = s2_ref[0, rows, :]
        for c, (w_ref, k) in enumerate(chunks):
            r = jnp.dot(hb, w_ref[:, k * MXU_COLS:(k + 1) * MXU_COLS],
                        preferred_element_type=jnp.float32)
            for half in range(MXU_COLS // LANES):
                j = c * (MXU_COLS // LANES) + half
                y = r[:, half * LANES:(half + 1) * LANES]
                if j in rope_slabs:
                    y = (y * cos + pltpu.roll(y, LANES - ROT_DIM // 2, 1) * s1
                         + pltpu.roll(y, ROT_DIM // 2, 1) * s2)
                out_ref[(0,) + slab_dst[j] + (rows, slice(None))] = y.astype(jnp.bfloat16)
        if with_forget:
            fa_ref[0, rows, :] = jnp.dot(hb, wf_ref[...], preferred_element_type=jnp.float32)


def _mod_spec(layer, which):
    return pl.BlockSpec((None, None, None, 1, D_MODEL), lambda bi, i: (layer, which, bi, 0, 0))


def _o_spec(part):
    return pl.BlockSpec((1, part.shape[1], TM, LANES), lambda bi, i: (bi, 0, i, 0))


def _layer_spec(shape, index):
    return pl.BlockSpec((None,) + tuple(shape[1:]), lambda bi, i: (index,) + (0,) * (len(shape) - 1))


def _in_projection(x, mod, layer, norm_pre, w, w_idx, wf, rope, rope_slabs, slab_dst, residual=None):
    b, s, d = x.shape
    with_forget = wf is not None
    with_residual = residual is not None
    row = lambda bi, i: (bi, i, 0)
    in_specs, args = [], []
    if with_residual:
        o, w_out, out_idx, norm_post, prev = residual
        in_specs += [_o_spec(part) for part in o]
        in_specs += [_layer_spec(w_out.shape, out_idx), _mod_spec(prev, 2), _layer_spec(norm_post.shape, prev)]
        args += list(o) + [w_out, mod, norm_post]
    in_specs += [pl.BlockSpec((1, TM, d), row), _mod_spec(layer, 0), _mod_spec(layer, 1),
                 _layer_spec(norm_pre.shape, layer)] + [_layer_spec(part.shape, w_idx) for part in w]
    args += [x, mod, mod, norm_pre] + list(w)
    if with_forget:
        in_specs.append(_layer_spec(wf.shape, w_idx))
        args.append(wf)
    in_specs += [pl.BlockSpec((1, TM, LANES), row)] * 3
    args += list(rope)
    out_specs, out_shape = [], []
    if with_residual:
        out_specs.append(pl.BlockSpec((1, TM, d), row))
        out_shape.append(jax.ShapeDtypeStruct((b, s, d), jnp.float32))
    out_specs.append(pl.BlockSpec((1, N_UNITS, N_KINDS, TM, LANES), lambda bi, i: (bi, 0, 0, i, 0)))
    out_shape.append(jax.ShapeDtypeStruct((b, N_UNITS, N_KINDS, s, LANES), jnp.bfloat16))
    if with_forget:
        out_specs.append(pl.BlockSpec((1, TM, LANES), row))
        out_shape.append(jax.ShapeDtypeStruct((b, s, LANES), jnp.float32))
    res = list(pl.pallas_call(
        functools.partial(_inproj_kernel, rope_slabs=frozenset(rope_slabs), slab_dst=tuple(slab_dst), n_w=len(w),
                          n_o=len(residual[0]) if with_residual else 0,
                          with_forget=with_forget, with_residual=with_residual),
        grid=(b, s // TM),
        in_specs=in_specs,
        out_specs=out_specs,
        out_shape=out_shape,
        compiler_params=_cparams(2),
        name="in_projection",
    )(*args))
    x_new = res.pop(0) if with_residual else x
    proj = res.pop(0)
    fa = res.pop(0) if with_forget else None
    return proj, fa, x_new


F_BLOCK = 256


def _fgate_kernel(fa_ref, b_ref, g_ref):
    s = fa_ref.shape[1]
    nh = N_HEADS_FOX
    z = fa_ref[0] + b_ref[...]
    logf = jnp.minimum(z, 0.0) - jnp.log(1.0 + jnp.exp(-jnp.abs(z)))
    r_i = lax.broadcasted_iota(jnp.int32, (F_BLOCK, F_BLOCK), 0)
    c_i = lax.broadcasted_iota(jnp.int32, (F_BLOCK, F_BLOCK), 1)
    tri = jnp.where(c_i <= r_i, 1.0, 0.0).astype(jnp.bfloat16)
    lane = lax.broadcasted_iota(jnp.int32, (1, LANES), 1)

    def split3(v):
        hi = v.astype(jnp.bfloat16)
        r1 = v - hi.astype(jnp.float32)
        mid = r1.astype(jnp.bfloat16)
        lo = (r1 - mid.astype(jnp.float32)).astype(jnp.bfloat16)
        return hi, mid, lo

    carry = jnp.zeros((1, LANES), jnp.float32)
    for i in range(s // F_BLOCK):
        rows = slice(i * F_BLOCK, (i + 1) * F_BLOCK)
        parts = jnp.concatenate(split3(logf[rows]), axis=1)
        cs3 = jnp.dot(tri, parts, preferred_element_type=jnp.float32)
        f = (cs3[:, :LANES] + cs3[:, LANES:2 * LANES]) + cs3[:, 2 * LANES:] + carry
        carry = f[F_BLOCK - 1:F_BLOCK, :]
        hi, mid, lo = split3(f * LOG2E)
        packed = (jnp.where(lane < nh, hi.astype(jnp.float32), 0.0)
                  + pltpu.roll(jnp.where(lane < nh, mid.astype(jnp.float32), 0.0), nh, 1)
                  + pltpu.roll(jnp.where(lane < nh, lo.astype(jnp.float32), 0.0), 2 * nh, 1))
        ones_hi = jnp.where((lane >= 3 * nh) & (lane < 6 * nh), 1.0, 0.0)
        ones_lo = jnp.where(lane < 3 * nh, 1.0, 0.0)
        g_ref[0, 0, 0, rows, :] = (ones_hi - packed).astype(jnp.bfloat16)
        g_ref[0, 0, 1, rows, :] = (ones_lo + pltpu.roll(packed, 3 * nh, 1)).astype(jnp.bfloat16)
    g_ref[0, 1] = jnp.zeros((2, s, LANES), jnp.bfloat16)


def _forget_prefix(fa, b_forget, idx):
    b, s, _ = fa.shape
    return pl.pallas_call(
        _fgate_kernel,
        grid=(b,),
        in_specs=[pl.BlockSpec((1, s, LANES), lambda i: (i, 0, 0)),
                  pl.BlockSpec((None, 1, LANES), lambda i: (idx, 0, 0))],
        out_specs=pl.BlockSpec((1, 2, 2, s, LANES), lambda i: (i, 0, 0, 0, 0)),
        out_shape=jax.ShapeDtypeStruct((b, 2, 2, s, LANES), jnp.bfloat16),
        compiler_params=_cparams(1),
        name="forget_prefix",
    )(fa, b_forget)


V_ROWS = LANES + 16


KPQ = TQ // TK


def _n_bias_tables(mode, s_len):
    if mode != "odd":
        return KPQ
    bounded = max((w // d) * d for (w, d) in DILATED_CONFIGS if (w // d) * d < s_len - 1)
    dist = 0
    while dist * TK - (TK - 1) <= bounded:
        dist += 1
    return dist + KPQ


LOG2_3 = math.log2(3.0)


def _fill_bias_tables(tab_ref, mode, s_len):
    n_tab = tab_ref.shape[0]
    kr = lax.broadcasted_iota(jnp.int32, (TK, TQ), 0)
    qc = lax.broadcasted_iota(jnp.int32, (TK, TQ), 1)
    for idx in range(n_tab):
        delta = (idx - (KPQ - 1)) * TK + qc - kr
        if mode != "odd":
            bias = jnp.where(delta >= 0, 0.0, NEG)
        else:
            mult = jnp.zeros((TK, TQ), jnp.int32)
            for (w, d) in DILATED_CONFIGS:
                reach = (w // d) * d
                if idx == n_tab - 1 and reach < s_len - 1:
                    continue
                ok = (delta >= 0) & (jnp.bitwise_and(delta, d - 1) == 0) & (delta <= reach)
                mult = mult + ok.astype(jnp.int32)
            bias = jnp.where(mult == 0, NEG, jnp.where(mult == 1, 0.0, jnp.where(mult == 2, 1.0, LOG2_3)))
        tab_ref[idx] = bias


def _attn_kernel(*refs, mode, lam_init, par_idx, unit_base):
    if mode == "fox":
        (p_ref, aug_ref, o_ref, vt, sb00, sb01, sb10, sb11, cm_buf, m1, a1, m2, a2, tab_ref) = refs
    elif mode == "diff":
        (p_ref, lq1_ref, lk1_ref, lq2_ref, lk2_ref, subln_ref, o_ref,
         vt, sb00, sb01, sb10, sb11, cm_buf, m1, a1, m2, a2, tab_ref) = refs
    else:
        (p_ref, o_ref, vt, sb00, sb01, sb10, sb11, cm_buf, m1, a1, m2, a2, tab_ref) = refs
    s_len = p_ref.shape[3]

    @pl.when((pl.program_id(0) == 0) & (pl.program_id(1) == 0))
    def _():
        _fill_bias_tables(tab_ref, mode, s_len)

    def unit(uu, carry):
        u = pl.program_id(1) * UNITS_PER_STEP + uu + unit_base
        lane = lax.broadcasted_iota(jnp.int32, (1, LANES), 1)
        lo = lane < HEAD_DIM
        nh = N_HEADS_FOX
        nt = (((1,), (1,)), ((), ()))

        vtt = p_ref[0, uu, 2].astype(jnp.float32).T
        ones_rows = jnp.where(lax.broadcasted_iota(jnp.int32, (V_ROWS - LANES, TK), 0) == 0, 1.0, 0.0)
        for j in range(s_len // TK):
            vt[j, 0:LANES, :] = vtt[:, j * TK:(j + 1) * TK].astype(jnp.bfloat16)
            vt[j, LANES:V_ROWS, :] = ones_rows.astype(jnp.bfloat16)

        state = ((m1, a1), (m2, a2))
        n_tab = tab_ref.shape[0]

        def make_lhs(qi):
            q0 = qi * TQ
            qs = p_ref[0, uu, 0, q0:q0 + TQ, :].astype(jnp.float32) * (HEAD_DIM ** -0.5 * LOG2E)
            lhs = [jnp.where(lo, qs, 0.0).astype(jnp.bfloat16),
                   jnp.where(lo, 0.0, qs).astype(jnp.bfloat16)]
            if mode == "fox":
                gq = aug_ref[0, 0, 1, q0:q0 + TQ, :].astype(jnp.float32)
                for i in range(2):
                    h = 2 * u + i
                    sel = (lane % nh == h) & (lane < 6 * nh)
                    qa = jnp.where(sel, gq, 0.0)
                    lhs[i] = jnp.concatenate([lhs[i], qa.astype(jnp.bfloat16)], axis=1)
            return lhs

        half = TK // 2
        assert TQ == TK
        s_bufs = ((sb00, sb01), (sb10, sb11))
        z0 = pl.multiple_of(jnp.minimum(pl.program_id(0), 0), TK)

        def scores_into(slot, lhs, qi, kt):
            k0 = kt * TK

            def keys(r0, n):
                kb = p_ref[0, uu, 1, r0:r0 + n, :]
                if mode == "fox":
                    kb = jnp.concatenate([kb, aug_ref[0, 0, 0, r0:r0 + n, :]], axis=1)
                return kb

            dist = qi - kt
            for i in range(2):
                if dist > 0:
                    st = lax.dot_general(keys(k0, TK), lhs[i], nt,
                                         preferred_element_type=jnp.float32)
                    if mode == "odd":
                        st = st + tab_ref[min(dist, n_tab - 1)]
                    s_bufs[slot][i][pl.ds(z0, TK), :] = st
                    cm_buf[slot, i] = jnp.max(st, axis=0, keepdims=True)
                else:
                    sa = lax.dot_general(keys(k0, half), lhs[i], nt,
                                         preferred_element_type=jnp.float32) + tab_ref[0, 0:half, :]
                    sb = lax.dot_general(keys(k0 + half, half), lhs[i][half:, :], nt,
                                         preferred_element_type=jnp.float32) + tab_ref[0, half:, half:]
                    s_bufs[slot][i][pl.ds(z0, half), :] = sa
                    s_bufs[slot][i][pl.ds(z0 + half, half), half:] = sb
                    ca = jnp.max(sa, axis=0, keepdims=True)
                    cb = jnp.max(sb, axis=0, keepdims=True)
                    cm_buf[slot, i] = jnp.concatenate(
                        [ca[:, :half], jnp.maximum(ca[:, half:], cb)], axis=1)

        def consume(slot, qi, kt):
            for i in range(2):
                m_r, a_r = state[i]
                m_prev = m_r[...]
                m_next = jnp.maximum(m_prev, cm_buf[slot, i])
                alpha = jnp.exp2(m_prev - m_next)
                if qi > kt:
                    pt = jnp.exp2(s_bufs[slot][i][pl.ds(z0, TK), :] - m_next).astype(jnp.bfloat16)
                    a_r[...] = alpha * a_r[...] + jnp.dot(vt[kt], pt, preferred_element_type=jnp.float32)
                else:
                    pa = jnp.exp2(s_bufs[slot][i][pl.ds(z0, half), :] - m_next).astype(jnp.bfloat16)
                    pb = jnp.exp2(s_bufs[slot][i][pl.ds(z0 + half, half), half:]
                                  - m_next[:, half:]).astype(jnp.bfloat16)
                    pva = jnp.dot(vt[kt, :, 0:half], pa, preferred_element_type=jnp.float32)
                    pvb = jnp.dot(vt[kt, :, half:], pb, preferred_element_type=jnp.float32)
                    a_r[:, 0:half] = alpha[:, :half] * a_r[:, 0:half] + pva[:, :half]
                    a_r[:, half:] = alpha[:, half:] * a_r[:, half:] + (pva[:, half:] + pvb)
                m_r[...] = m_next

        def finish(qi):
            q0 = qi * TQ
            outs = []
            for (m_r, a_r) in state:
                acc = a_r[...]
                inv_l = 1.0 / acc[LANES:LANES + 1, :]
                outs.append((acc[0:LANES, :] * inv_l).T)
            o1, o2 = outs
            g = p_ref[0, uu, 3, q0:q0 + TQ, :].astype(jnp.float32)
            gate = g * jax.nn.sigmoid(g)
            if mode == "diff":
                pr = slice(par_idx, par_idx + 1)
                lam = (jnp.exp(jnp.sum(lq1_ref[pr, :] * lk1_ref[pr, :], axis=1, keepdims=True))
                       - jnp.exp(jnp.sum(lq2_ref[pr, :] * lk2_ref[pr, :], axis=1, keepdims=True)) + lam_init)
                d = o1 - lam * o2
                var = jnp.mean(d * d, axis=-1, keepdims=True)
                out = (d * lax.rsqrt(var + NORM_EPS) * subln_ref[pr, :]) * (1.0 - lam_init)
            else:
                out = jnp.where(lo, o1, o2)
            o_ref[0, uu, q0:q0 + TQ, :] = (out * gate).astype(jnp.bfloat16)

        pairs = [(qi, kt) for qi in range(s_len // TQ) for kt in range(qi + 1)]
        lhs = make_lhs(0)
        scores_into(0, lhs, 0, 0)
        for n, (qi, kt) in enumerate(pairs):
            if kt == 0:
                for (m_r, a_r) in state:
                    m_r[...] = jnp.full((1, TQ), NEG, jnp.float32)
                    a_r[...] = jnp.zeros((V_ROWS, TQ), jnp.float32)
            if n + 1 < len(pairs):
                nqi, nkt = pairs[n + 1]
                if nkt == 0:
                    lhs = make_lhs(nqi)
                scores_into((n + 1) % 2, lhs, nqi, nkt)
            consume(n % 2, qi, kt)
            if kt == qi:
                finish(qi)
        return carry

    lax.fori_loop(0, UNITS_PER_STEP, unit, 0)


def _attention(proj, mode, aug=None, params=None, par_idx=0, lam_init=0.0):
    b, _, _, s, _ = proj.shape
    n_units = N_UNITS if mode == "odd" else N_UNITS // 2
    unit_base = N_UNITS // 2 if mode == "diff" else 0
    base_block = unit_base // UNITS_PER_STEP
    in_specs = [pl.BlockSpec((1, UNITS_PER_STEP, N_KINDS, s, LANES), lambda bi, p: (bi, base_block + p, 0, 0, 0))]
    args = [proj]
    if mode == "fox":
        in_specs.append(pl.BlockSpec((1, 1, 2, s, LANES), lambda bi, p: (bi, 0, 0, 0, 0)))
        args.append(aug)
    if mode == "diff":
        for arr in params:
            in_specs.append(pl.BlockSpec(arr.shape, lambda bi, p: (0, 0)))
            args.append(arr)
    scratch = [pltpu.VMEM((s // TK, V_ROWS, TK), jnp.bfloat16)]
    scratch += [pltpu.VMEM((TK, TQ), jnp.float32)] * 4
    scratch.append(pltpu.VMEM((2, 2, 1, TQ), jnp.float32))
    scratch += [pltpu.VMEM((1, TQ), jnp.float32), pltpu.VMEM((V_ROWS, TQ), jnp.float32)] * 2
    scratch.append(pltpu.VMEM((_n_bias_tables(mode, s), TK, TQ), jnp.float32))
    return pl.pallas_call(
        functools.partial(_attn_kernel, mode=mode, lam_init=lam_init, par_idx=par_idx, unit_base=unit_base),
        grid=(b, n_units // UNITS_PER_STEP),
        in_specs=in_specs,
        out_specs=pl.BlockSpec((1, UNITS_PER_STEP, s, LANES), lambda bi, p: (bi, p, 0, 0)),
        out_shape=jax.ShapeDtypeStruct((b, n_units, s, LANES), jnp.bfloat16),
        scratch_shapes=scratch,
        compiler_params=_cparams(2),
        name="attention_" + mode,
    )(*args)


def _outproj_kernel(*refs):
    o_refs, (w_ref, x_ref, gate_ref, npost_ref, out_ref) = refs[:-5], refs[-5:]
    out_ref[0] = _residual_update(o_refs, w_ref, slice(None), x_ref[0], gate_ref[...], npost_ref[...])


def _out_projection(o, w_out, out_idx, x, mod, layer, norm_post):
    b, s, d = x.shape
    return pl.pallas_call(
        _outproj_kernel,
        grid=(b, s // TM),
        in_specs=[_o_spec(part) for part in o] + [
            _layer_spec(w_out.shape, out_idx),
            pl.BlockSpec((1, TM, d), lambda bi, i: (bi, i, 0)),
            _mod_spec(layer, 2),
            _layer_spec(norm_post.shape, layer),
        ],
        out_specs=pl.BlockSpec((1, TM, d), lambda bi, i: (bi, i, 0)),
        out_shape=jax.ShapeDtypeStruct((b, s, d), jnp.float32),
        compiler_params=_cparams(2),
        name="out_projection",
    )(*o, w_out, x, mod, norm_post)


def _even_weight_layout(w_in):
    f0 = 3 * N_HEADS_FOX * HEAD_DIM
    f1 = f0 + N_HEADS_FOX
    parts = (w_in[..., :f0].astype(jnp.bfloat16), w_in[..., f1:].astype(jnp.bfloat16))
    wf = jnp.pad(w_in[..., f0:f1], ((0, 0), (0, 0), (0, LANES - N_HEADS_FOX))).astype(jnp.bfloat16)
    return parts, wf


def kernel(x, c, positions, norm_pre, norm_post, ada_w, ada_b, ev_w_in, ev_b_forget,
           ev_lambda_q1, ev_lambda_k1, ev_lambda_q2, ev_lambda_k2, ev_subln, ev_w_out,
           od_w_in, od_w_out):
    depth, d = norm_pre.shape
    mod = _adaln_mod(c, ada_w, ada_b)
    rope = _rope_tables(positions)
    half = N_UNITS // 2
    even_dst = ([(j % half, j // half) for j in range(N_KINDS * half)]
                + [(half + j % half, j // half) for j in range(N_KINDS * half)])
    odd_dst = [(j % N_UNITS, j // N_UNITS) for j in range(N_SLABS)]
    even_rope = set(range(4 * half, 6 * half))
    odd_rope = set(range(2 * N_UNITS))
    ev_w_main, ev_wf = _even_weight_layout(ev_w_in)
    od_w_main = (od_w_in.astype(jnp.bfloat16),)
    w_outs = (ev_w_out.astype(jnp.bfloat16), od_w_out.astype(jnp.bfloat16))
    npre = norm_pre.reshape(depth, 1, d)
    npost = norm_post.reshape(depth, 1, d)
    b_forget = jnp.pad(ev_b_forget, ((0, 0), (0, LANES - N_HEADS_FOX))).reshape(-1, 1, LANES)
    lam_params = (ev_lambda_q1, ev_lambda_k1, ev_lambda_q2, ev_lambda_k2, ev_subln)
    pending = None
    for layer in range(depth):
        idx = layer // 2
        if layer % 2 == 0:
            proj, fa, x = _in_projection(x, mod, layer, npre, ev_w_main, idx, ev_wf, rope, even_rope, even_dst,
                                         pending)
            aug = _forget_prefix(fa, b_forget, idx)
            lam_init = 0.8 - 0.6 * math.exp(-0.3 * layer)
            o = (_attention(proj, "fox", aug=aug),
                 _attention(proj, "diff", params=lam_params, par_idx=idx, lam_init=lam_init))
        else:
            proj, _, x = _in_projection(x, mod, layer, npre, od_w_main, idx, None, rope, odd_rope, odd_dst,
                                        pending)
            o = (_attention(proj, "odd"),)
        pending = (o, w_outs[layer % 2], idx, npost, layer)
    o, w_out, idx, _, layer = pending
    return _out_projection(o, w_out, idx, x, mod, layer, npost)
```

```python
import functools
import math

import jax
import jax.numpy as jnp
from jax import lax
from jax.experimental import pallas as pl
from jax.experimental.pallas import tpu as pltpu

D_MODEL = 1024
HEAD_DIM = 64
N_HEADS_FOX = 8
ROT_DIM = HEAD_DIM // 4
ROPE_THETA = 500000.0
NORM_EPS = 1e-6
DILATED_CONFIGS = ((128, 1), (512, 4), (2048, 16))

LANES = 128
MXU_COLS = 256
N_SLABS = 32
N_UNITS = 8
N_KINDS = 4
UNITS_PER_STEP = 1
TQ = 512
TK = 512
TM = 512
NEG = -1e30
LOG2E = math.log2(math.e)
VMEM_LIMIT = 48 * 1024 * 1024


def _cparams(n_axes):
    return pltpu.CompilerParams(
        dimension_semantics=("arbitrary",) * n_axes, vmem_limit_bytes=VMEM_LIMIT)


def _mod_kernel(c_ref, w_ref, b_ref, o_ref):
    c = c_ref[...]
    cond = c * jax.nn.sigmoid(c)
    for j in range(3):
        o_ref[0, j] = (jnp.dot(cond, w_ref[0, :, j * D_MODEL:(j + 1) * D_MODEL],
                               preferred_element_type=jnp.float32) + b_ref[0, j])


def _adaln_mod(c, ada_w, ada_b):
    depth = ada_w.shape[0]
    b = c.shape[0]
    out = pl.pallas_call(
        _mod_kernel,
        grid=(depth,),
        in_specs=[
            pl.BlockSpec((b, D_MODEL), lambda l: (0, 0)),
            pl.BlockSpec((1, D_MODEL, 3 * D_MODEL), lambda l: (l, 0, 0)),
            pl.BlockSpec((1, 3, 1, D_MODEL), lambda l: (l, 0, 0, 0)),
        ],
        out_specs=pl.BlockSpec((1, 3, b, D_MODEL), lambda l: (l, 0, 0, 0)),
        out_shape=jax.ShapeDtypeStruct((depth, 3, b, D_MODEL), jnp.float32),
        compiler_params=_cparams(1),
        name="adaln_mod",
    )(c, ada_w, ada_b.reshape(depth, 3, 1, D_MODEL))
    return out.reshape(depth, 3, b, 1, D_MODEL)


ROPE_PACK = LANES // (ROT_DIM // 2)


def _rope_kernel(pos_ref, c_ref, s1_ref, s2_ref):
    nf = ROT_DIM // 2
    pos = pos_ref[0].astype(jnp.float32)
    lane = lax.broadcasted_iota(jnp.int32, (1, LANES), 1)
    fidx = (lane % nf).astype(jnp.float32)
    theta = jnp.full((1, LANES), ROPE_THETA, jnp.float32)
    inv_freq = jnp.exp(-(fidx * (2.0 / ROT_DIM)) * jnp.log(theta))
    ang = pos * inv_freq
    n_rows = pos.shape[0]

    def split3(v):
        hi = v.astype(jnp.bfloat16)
        r1 = v - hi.astype(jnp.float32)
        mid = r1.astype(jnp.bfloat16)
        lo = (r1 - mid.astype(jnp.float32)).astype(jnp.bfloat16)
        return jnp.concatenate([hi, mid, lo], axis=0)

    cos3 = split3(jnp.cos(ang))
    sin3 = split3(jnp.sin(ang))
    src = lax.broadcasted_iota(jnp.int32, (LANES, LANES), 0)
    dst = lax.broadcasted_iota(jnp.int32, (LANES, LANES), 1)
    cc = dst % HEAD_DIM
    for j in range(ROPE_PACK):
        hit = (src // nf == j) & (src % nf == cc % nf)
        e_c = jnp.where(hit & (cc < ROT_DIM), 1.0, 0.0).astype(jnp.bfloat16)
        e_1 = jnp.where(hit & (cc < nf), -1.0, 0.0).astype(jnp.bfloat16)
        e_2 = jnp.where(hit & (cc >= nf) & (cc < ROT_DIM), 1.0, 0.0).astype(jnp.bfloat16)
        rows = pl.ds(j, n_rows, stride=ROPE_PACK)
        for ref, tab3, e, base in ((c_ref, cos3, e_c, 1.0), (s1_ref, sin3, e_1, 0.0), (s2_ref, sin3, e_2, 0.0)):
            t3 = jnp.dot(tab3, e, preferred_element_type=jnp.float32)
            t = (t3[0:n_rows] + t3[n_rows:2 * n_rows]) + t3[2 * n_rows:]
            if base:
                t = t + jnp.where(dst[0:1] % HEAD_DIM < ROT_DIM, 0.0, base)
            ref[0, rows, :] = t


def _rope_tables(positions):
    b, s = positions.shape
    packed = jnp.repeat(positions.reshape(b, s // ROPE_PACK, ROPE_PACK), ROT_DIM // 2, axis=-1)
    spec = pl.BlockSpec((1, s, LANES), lambda i: (i, 0, 0))
    shp = jax.ShapeDtypeStruct((b, s, LANES), jnp.float32)
    return pl.pallas_call(
        _rope_kernel,
        grid=(b,),
        in_specs=[pl.BlockSpec((1, s // ROPE_PACK, LANES), lambda i: (i, 0, 0))],
        out_specs=[spec, spec, spec],
        out_shape=[shp, shp, shp],
        compiler_params=_cparams(1),
        name="rope_tables",
    )(packed)


def _residual_update(o_refs, wout_ref, rows, x, gate, npost):
    cat = jnp.concatenate([o_ref[0, j, rows, :] for o_ref in o_refs for j in range(o_ref.shape[1])],
                          axis=1)
    y = jnp.dot(cat, wout_ref[...], preferred_element_type=jnp.float32)
    var = jnp.mean(y * y, axis=-1, keepdims=True)
    return x + gate * (y * lax.rsqrt(var + NORM_EPS) * npost)


SUB_TILES = 2


def _inproj_kernel(*refs, rope_slabs, slab_dst, n_w, n_o, with_forget, with_residual):
    refs = list(refs)
    if with_residual:
        o_refs, refs = refs[:n_o], refs[n_o:]
        wout_ref, gate_ref, npost_ref = refs[:3]
        refs = refs[3:]
    x_ref, shift_ref, scale_ref, npre_ref = refs[:4]
    w_refs, refs = refs[4:4 + n_w], refs[4 + n_w:]
    wf_ref = refs.pop(0) if with_forget else None
    c_ref, s1_ref, s2_ref = refs[:3]
    refs = refs[3:]
    xout_ref = refs.pop(0) if with_residual else None
    out_ref = refs.pop(0)
    fa_ref = refs.pop(0) if with_forget else None

    tm = x_ref.shape[1]
    sub_rows = [slice(t * tm // SUB_TILES, (t + 1) * tm // SUB_TILES) for t in range(SUB_TILES)]
    hbs = []
    for rows in sub_rows:
        x = x_ref[0, rows, :]
        if with_residual:
            x = _residual_update(o_refs, wout_ref, rows, x, gate_ref[...], npost_ref[...])
            xout_ref[0, rows, :] = x
        var = jnp.mean(x * x, axis=-1, keepdims=True)
        h = (x * lax.rsqrt(var + NORM_EPS) * npre_ref[...]) * (1.0 + scale_ref[...]) + shift_ref[...]
        hbs.append(h.astype(jnp.bfloat16))
    chunks = [(w_ref, k) for w_ref in w_refs for k in range(w_ref.shape[1] // MXU_COLS)]
    assert len(chunks) == N_SLABS * LANES // MXU_COLS
    for rows, hb in zip(sub_rows, hbs):
        cos = c_ref[0, rows, :]
        s1 = s1_ref[0, rows, :]
        s2 = s2_ref[0, rows, :]
        for c, (w_ref, k) in enumerate(chunks):
            r = jnp.dot(hb, w_ref[:, k * MXU_COLS:(k + 1) * MXU_COLS],
                        preferred_element_type=jnp.float32)
            for half in range(MXU_COLS // LANES):
                j = c * (MXU_COLS // LANES) + half
                y = r[:, half * LANES:(half + 1) * LANES]
                if j in rope_slabs:
                    y = (y * cos + pltpu.roll(y, LANES - ROT_DIM // 2, 1) * s1
                         + pltpu.roll(y, ROT_DIM // 2, 1) * s2)
                out_ref[(0,) + slab_dst[j] + (rows, slice(None))] = y.astype(jnp.bfloat16)
        if with_forget:
            fa_ref[0, rows, :] = jnp.dot(hb, wf_ref[...], preferred_element_type=jnp.float32)


def _mod_spec(layer, which):
    return pl.BlockSpec((None, None, None, 1, D_MODEL), lambda bi, i: (layer, which, bi, 0, 0))


def _o_spec(part):
    return pl.BlockSpec((1, part.shape[1], TM, LANES), lambda bi, i: (bi, 0, i, 0))


def _layer_spec(shape, index):
    return pl.BlockSpec((None,) + tuple(shape[1:]), lambda bi, i: (index,) + (0,) * (len(shape) - 1))


def _in_projection(x, mod, layer, norm_pre, w, w_idx, wf, rope, rope_slabs, slab_dst, residual=None):
    b, s, d = x.shape
    with_forget = wf is not None
    with_residual = residual is not None
    row = lambda bi, i: (bi, i, 0)
    in_specs, args = [], []
    if with_residual:
        o, w_out, out_idx, norm_post, prev = residual
        in_specs += [_o_spec(part) for part in o]
        in_specs += [_layer_spec(w_out.shape, out_idx), _mod_spec(prev, 2), _layer_spec(norm_post.shape, prev)]
        args += list(o) + [w_out, mod, norm_post]
    in_specs += [pl.BlockSpec((1, TM, d), row), _mod_spec(layer, 0), _mod_spec(layer, 1),
                 _layer_spec(norm_pre.shape, layer)] + [_layer_spec(part.shape, w_idx) for part in w]
    args += [x, mod, mod, norm_pre] + list(w)
    if with_forget:
        in_specs.append(_layer_spec(wf.shape, w_idx))
        args.append(wf)
    in_specs += [pl.BlockSpec((1, TM, LANES), row)] * 3
    args += list(rope)
    out_specs, out_shape = [], []
    if with_residual:
        out_specs.append(pl.BlockSpec((1, TM, d), row))
        out_shape.append(jax.ShapeDtypeStruct((b, s, d), jnp.float32))
    out_specs.append(pl.BlockSpec((1, N_UNITS, N_KINDS, TM, LANES), lambda bi, i: (bi, 0, 0, i, 0)))
    out_shape.append(jax.ShapeDtypeStruct((b, N_UNITS, N_KINDS, s, LANES), jnp.bfloat16))
    if with_forget:
        out_specs.append(pl.BlockSpec((1, TM, LANES), row))
        out_shape.append(jax.ShapeDtypeStruct((b, s, LANES), jnp.float32))
    res = list(pl.pallas_call(
        functools.partial(_inproj_kernel, rope_slabs=frozenset(rope_slabs), slab_dst=tuple(slab_dst), n_w=len(w),
                          n_o=len(residual[0]) if with_residual else 0,
                          with_forget=with_forget, with_residual=with_residual),
        grid=(b, s // TM),
        in_specs=in_specs,
        out_specs=out_specs,
        out_shape=out_shape,
        compiler_params=_cparams(2),
        name="in_projection",
    )(*args))
    x_new = res.pop(0) if with_residual else x
    proj = res.pop(0)
    fa = res.pop(0) if with_forget else None
    return proj, fa, x_new


F_BLOCK = 256


def _fgate_kernel(fa_ref, b_ref, g_ref):
    s = fa_ref.shape[1]
    nh = N_HEADS_FOX
    z = fa_ref[0] + b_ref[...]
    logf = jnp.minimum(z, 0.0) - jnp.log(1.0 + jnp.exp(-jnp.abs(z)))
    r_i = lax.broadcasted_iota(jnp.int32, (F_BLOCK, F_BLOCK), 0)
    c_i = lax.broadcasted_iota(jnp.int32, (F_BLOCK, F_BLOCK), 1)
    tri = jnp.where(c_i <= r_i, 1.0, 0.0).astype(jnp.bfloat16)
    lane = lax.broadcasted_iota(jnp.int32, (1, LANES), 1)

    def split3(v):
        hi = v.astype(jnp.bfloat16)
        r1 = v - hi.astype(jnp.float32)
        mid = r1.astype(jnp.bfloat16)
        lo = (r1 - mid.astype(jnp.float32)).astype(jnp.bfloat16)
        return hi, mid, lo

    carry = jnp.zeros((1, LANES), jnp.float32)
    for i in range(s // F_BLOCK):
        rows = slice(i * F_BLOCK, (i + 1) * F_BLOCK)
        parts = jnp.concatenate(split3(logf[rows]), axis=1)
        cs3 = jnp.dot(tri, parts, preferred_element_type=jnp.float32)
        f = (cs3[:, :LANES] + cs3[:, LANES:2 * LANES]) + cs3[:, 2 * LANES:] + carry
        carry = f[F_BLOCK - 1:F_BLOCK, :]
        hi, mid, lo = split3(f * LOG2E)
        packed = (jnp.where(lane < nh, hi.astype(jnp.float32), 0.0)
                  + pltpu.roll(jnp.where(lane < nh, mid.astype(jnp.float32), 0.0), nh, 1)
                  + pltpu.roll(jnp.where(lane < nh, lo.astype(jnp.float32), 0.0), 2 * nh, 1))
        ones_hi = jnp.where((lane >= 3 * nh) & (lane < 6 * nh), 1.0, 0.0)
        ones_lo = jnp.where(lane < 3 * nh, 1.0, 0.0)
        g_ref[0, 0, rows, :] = (ones_hi - packed).astype(jnp.bfloat16)
        g_ref[0, 1, rows, :] = (ones_lo + pltpu.roll(packed, 3 * nh, 1)).astype(jnp.bfloat16)


def _forget_prefix(fa, b_forget, idx):
    b, s, _ = fa.shape
    return pl.pallas_call(
        _fgate_kernel,
        grid=(b,),
        in_specs=[pl.BlockSpec((1, s, LANES), lambda i: (i, 0, 0)),
                  pl.BlockSpec((None, 1, LANES), lambda i: (idx, 0, 0))],
        out_specs=pl.BlockSpec((1, 2, s, LANES), lambda i: (i, 0, 0, 0)),
        out_shape=jax.ShapeDtypeStruct((b, 2, s, LANES), jnp.bfloat16),
        compiler_params=_cparams(1),
        name="forget_prefix",
    )(fa, b_forget)


V_ROWS = LANES + 16


KPQ = TQ // TK


def _n_bias_tables(mode, s_len):
    if mode != "odd":
        return KPQ
    bounded = max((w // d) * d for (w, d) in DILATED_CONFIGS if (w // d) * d < s_len - 1)
    dist = 0
    while dist * TK - (TK - 1) <= bounded:
        dist += 1
    return dist + KPQ


LOG2_3 = math.log2(3.0)


def _fill_bias_tables(tab_ref, mode, s_len):
    n_tab = tab_ref.shape[0]
    kr = lax.broadcasted_iota(jnp.int32, (TK, TQ), 0)
    qc = lax.broadcasted_iota(jnp.int32, (TK, TQ), 1)
    for idx in range(n_tab):
        delta = (idx - (KPQ - 1)) * TK + qc - kr
        if mode != "odd":
            bias = jnp.where(delta >= 0, 0.0, NEG)
        else:
            mult = jnp.zeros((TK, TQ), jnp.int32)
            for (w, d) in DILATED_CONFIGS:
                reach = (w // d) * d
                if idx == n_tab - 1 and reach < s_len - 1:
                    continue
                ok = (delta >= 0) & (jnp.bitwise_and(delta, d - 1) == 0) & (delta <= reach)
                mult = mult + ok.astype(jnp.int32)
            bias = jnp.where(mult == 0, NEG, jnp.where(mult == 1, 0.0, jnp.where(mult == 2, 1.0, LOG2_3)))
        tab_ref[idx] = bias


def _attn_kernel(*refs, mode, lam_init, par_idx, unit_base):
    if mode == "fox":
        (p_ref, aug_ref, o_ref, vt, sb00, sb01, sb10, sb11, cm_buf, m1, a1, m2, a2, tab_ref) = refs
    elif mode == "diff":
        (p_ref, lq1_ref, lk1_ref, lq2_ref, lk2_ref, subln_ref, o_ref,
         vt, sb00, sb01, sb10, sb11, cm_buf, m1, a1, m2, a2, tab_ref) = refs
    else:
        (p_ref, o_ref, vt, sb00, sb01, sb10, sb11, cm_buf, m1, a1, m2, a2, tab_ref) = refs
    s_len = p_ref.shape[3]

    @pl.when((pl.program_id(0) == 0) & (pl.program_id(1) == 0))
    def _():
        _fill_bias_tables(tab_ref, mode, s_len)

    def unit(uu, carry):
        u = pl.program_id(1) * UNITS_PER_STEP + uu + unit_base
        lane = lax.broadcasted_iota(jnp.int32, (1, LANES), 1)
        lo = lane < HEAD_DIM
        nh = N_HEADS_FOX
        nt = (((1,), (1,)), ((), ()))

        vtt = p_ref[0, uu, 2].astype(jnp.float32).T
        ones_rows = jnp.where(lax.broadcasted_iota(jnp.int32, (V_ROWS - LANES, TK), 0) == 0, 1.0, 0.0)
        for j in range(s_len // TK):
            vt[j, 0:LANES, :] = vtt[:, j * TK:(j + 1) * TK].astype(jnp.bfloat16)
            vt[j, LANES:V_ROWS, :] = ones_rows.astype(jnp.bfloat16)

        state = ((m1, a1), (m2, a2))
        n_tab = tab_ref.shape[0]

        def make_lhs(qi):
            q0 = qi * TQ
            qs = p_ref[0, uu, 0, q0:q0 + TQ, :].astype(jnp.float32) * (HEAD_DIM ** -0.5 * LOG2E)
            lhs = [jnp.where(lo, qs, 0.0).astype(jnp.bfloat16),
                   jnp.where(lo, 0.0, qs).astype(jnp.bfloat16)]
            if mode == "fox":
                gq = aug_ref[0, 1, q0:q0 + TQ, :].astype(jnp.float32)
                for i in range(2):
                    h = 2 * u + i
                    sel = (lane % nh == h) & (lane < 6 * nh)
                    qa = jnp.where(sel, gq, 0.0)
                    lhs[i] = jnp.concatenate([lhs[i], qa.astype(jnp.bfloat16)], axis=1)
            return lhs

        half = TK // 2
        assert TQ == TK
        s_bufs = ((sb00, sb01), (sb10, sb11))
        z0 = pl.multiple_of(jnp.minimum(pl.program_id(0), 0), TK)

        def scores_into(slot, lhs, qi, kt):
            k0 = kt * TK

            def keys(r0, n):
                kb = p_ref[0, uu, 1, r0:r0 + n, :]
                if mode == "fox":
                    kb = jnp.concatenate([kb, aug_ref[0, 0, r0:r0 + n, :]], axis=1)
                return kb

            dist = qi - kt
            for i in range(2):
                if dist > 0:
                    st = lax.dot_general(keys(k0, TK), lhs[i], nt,
                                         preferred_element_type=jnp.float32)
                    if mode == "odd":
                        st = st + tab_ref[min(dist, n_tab - 1)]
                    s_bufs[slot][i][pl.ds(z0, TK), :] = st
                    cm_buf[slot, i] = jnp.max(st, axis=0, keepdims=True)
                else:
                    sa = lax.dot_general(keys(k0, half), lhs[i], nt,
                                         preferred_element_type=jnp.float32) + tab_ref[0, 0:half, :]
                    sb = lax.dot_general(keys(k0 + half, half), lhs[i][half:, :], nt,
                                         preferred_element_type=jnp.float32) + tab_ref[0, half:, half:]
                    s_bufs[slot][i][pl.ds(z0, half), :] = sa
                    s_bufs[slot][i][pl.ds(z0 + half, half), half:] = sb
                    ca = jnp.max(sa, axis=0, keepdims=True)
                    cb = jnp.max(sb, axis=0, keepdims=True)
                    cm_buf[slot, i] = jnp.concatenate(
                        [ca[:, :half], jnp.maximum(ca[:, half:], cb)], axis=1)

        def consume(slot, qi, kt):
            for i in range(2):
                m_r, a_r = state[i]
                m_prev = m_r[...]
                m_next = jnp.maximum(m_prev, cm_buf[slot, i])
                alpha = jnp.exp2(m_prev - m_next)
                if qi > kt:
                    pt = jnp.exp2(s_bufs[slot][i][pl.ds(z0, TK), :] - m_next).astype(jnp.bfloat16)
                    a_r[...] = alpha * a_r[...] + jnp.dot(vt[kt], pt, preferred_element_type=jnp.float32)
                else:
                    pa = jnp.exp2(s_bufs[slot][i][pl.ds(z0, half), :] - m_next).astype(jnp.bfloat16)
                    pb = jnp.exp2(s_bufs[slot][i][pl.ds(z0 + half, half), half:]
                                  - m_next[:, half:]).astype(jnp.bfloat16)
                    pva = jnp.dot(vt[kt, :, 0:half], pa, preferred_element_type=jnp.float32)
                    pvb = jnp.dot(vt[kt, :, half:], pb, preferred_element_type=jnp.float32)
                    a_r[:, 0:half] = alpha[:, :half] * a_r[:, 0:half] + pva[:, :half]
                    a_r[:, half:] = alpha[:, half:] * a_r[:, half:] + (pva[:, half:] + pvb)
                m_r[...] = m_next

        def finish(qi):
            q0 = qi * TQ
            outs = []
            for (m_r, a_r) in state:
                acc = a_r[...]
                inv_l = 1.0 / acc[LANES:LANES + 1, :]
                outs.append((acc[0:LANES, :] * inv_l).T)
            o1, o2 = outs
            g = p_ref[0, uu, 3, q0:q0 + TQ, :].astype(jnp.float32)
            gate = g * jax.nn.sigmoid(g)
            if mode == "diff":
                pr = slice(par_idx, par_idx + 1)
                lam = (jnp.exp(jnp.sum(lq1_ref[pr, :] * lk1_ref[pr, :], axis=1, keepdims=True))
                       - jnp.exp(jnp.sum(lq2_ref[pr, :] * lk2_ref[pr, :], axis=1, keepdims=True)) + lam_init)
                d = o1 - lam * o2
                var = jnp.mean(d * d, axis=-1, keepdims=True)
                out = (d * lax.rsqrt(var + NORM_EPS) * subln_ref[pr, :]) * (1.0 - lam_init)
            else:
                out = jnp.where(lo, o1, o2)
            o_ref[0, uu, q0:q0 + TQ, :] = (out * gate).astype(jnp.bfloat16)

        pairs = [(qi, kt) for qi in range(s_len // TQ) for kt in range(qi + 1)]
        lhs = make_lhs(0)
        scores_into(0, lhs, 0, 0)
        for n, (qi, kt) in enumerate(pairs):
            if kt == 0:
                for (m_r, a_r) in state:
                    m_r[...] = jnp.full((1, TQ), NEG, jnp.float32)
                    a_r[...] = jnp.zeros((V_ROWS, TQ), jnp.float32)
            if n + 1 < len(pairs):
                nqi, nkt = pairs[n + 1]
                if nkt == 0:
                    lhs = make_lhs(nqi)
                scores_into((n + 1) % 2, lhs, nqi, nkt)
            consume(n % 2, qi, kt)
            if kt == qi:
                finish(qi)
        return carry

    lax.fori_loop(0, UNITS_PER_STEP, unit, 0)


def _attention(proj, mode, aug=None, params=None, par_idx=0, lam_init=0.0):
    b, _, _, s, _ = proj.shape
    n_units = N_UNITS if mode == "odd" else N_UNITS // 2
    unit_base = N_UNITS // 2 if mode == "diff" else 0
    base_block = unit_base // UNITS_PER_STEP
    in_specs = [pl.BlockSpec((1, UNITS_PER_STEP, N_KINDS, s, LANES), lambda bi, p: (bi, base_block + p, 0, 0, 0))]
    args = [proj]
    if mode == "fox":
        in_specs.append(pl.BlockSpec((1, 2, s, LANES), lambda bi, p: (bi, 0, 0, 0)))
        args.append(aug)
    if mode == "diff":
        for arr in params:
            in_specs.append(pl.BlockSpec(arr.shape, lambda bi, p: (0, 0)))
            args.append(arr)
    scratch = [pltpu.VMEM((s // TK, V_ROWS, TK), jnp.bfloat16)]
    scratch += [pltpu.VMEM((TK, TQ), jnp.float32)] * 4
    scratch.append(pltpu.VMEM((2, 2, 1, TQ), jnp.float32))
    scratch += [pltpu.VMEM((1, TQ), jnp.float32), pltpu.VMEM((V_ROWS, TQ), jnp.float32)] * 2
    scratch.append(pltpu.VMEM((_n_bias_tables(mode, s), TK, TQ), jnp.float32))
    return pl.pallas_call(
        functools.partial(_attn_kernel, mode=mode, lam_init=lam_init, par_idx=par_idx, unit_base=unit_base),
        grid=(b, n_units // UNITS_PER_STEP),
        in_specs=in_specs,
        out_specs=pl.BlockSpec((1, UNITS_PER_STEP, s, LANES), lambda bi, p: (bi, p, 0, 0)),
        out_shape=jax.ShapeDtypeStruct((b, n_units, s, LANES), jnp.bfloat16),
        scratch_shapes=scratch,
        compiler_params=_cparams(2),
        name="attention_" + mode,
    )(*args)


def _outproj_kernel(*refs):
    o_refs, (w_ref, x_ref, gate_ref, npost_ref, out_ref) = refs[:-5], refs[-5:]
    out_ref[0] = _residual_update(o_refs, w_ref, slice(None), x_ref[0], gate_ref[...], npost_ref[...])


def _out_projection(o, w_out, out_idx, x, mod, layer, norm_post):
    b, s, d = x.shape
    return pl.pallas_call(
        _outproj_kernel,
        grid=(b, s // TM),
        in_specs=[_o_spec(part) for part in o] + [
            _layer_spec(w_out.shape, out_idx),
            pl.BlockSpec((1, TM, d), lambda bi, i: (bi, i, 0)),
            _mod_spec(layer, 2),
            _layer_spec(norm_post.shape, layer),
        ],
        out_specs=pl.BlockSpec((1, TM, d), lambda bi, i: (bi, i, 0)),
        out_shape=jax.ShapeDtypeStruct((b, s, d), jnp.float32),
        compiler_params=_cparams(2),
        name="out_projection",
    )(*o, w_out, x, mod, norm_post)


def _even_weight_layout(w_in):
    f0 = 3 * N_HEADS_FOX * HEAD_DIM
    f1 = f0 + N_HEADS_FOX
    w16 = w_in.astype(jnp.bfloat16)
    parts = (w16[..., :f0], w16[..., f1:])
    wf = jnp.pad(w16[..., f0:f1], ((0, 0), (0, 0), (0, LANES - N_HEADS_FOX)))
    return parts, wf


def kernel(x, c, positions, norm_pre, norm_post, ada_w, ada_b, ev_w_in, ev_b_forget,
           ev_lambda_q1, ev_lambda_k1, ev_lambda_q2, ev_lambda_k2, ev_subln, ev_w_out,
           od_w_in, od_w_out):
    depth, d = norm_pre.shape
    mod = _adaln_mod(c, ada_w, ada_b)
    rope = _rope_tables(positions)
    half = N_UNITS // 2
    even_dst = ([(j % half, j // half) for j in range(N_KINDS * half)]
                + [(half + j % half, j // half) for j in range(N_KINDS * half)])
    odd_dst = [(j % N_UNITS, j // N_UNITS) for j in range(N_SLABS)]
    even_rope = set(range(4 * half, 6 * half))
    odd_rope = set(range(2 * N_UNITS))
    ev_w_main, ev_wf = _even_weight_layout(ev_w_in)
    od_w_main = (od_w_in.astype(jnp.bfloat16),)
    w_outs = (ev_w_out.astype(jnp.bfloat16), od_w_out.astype(jnp.bfloat16))
    npre = norm_pre.reshape(depth, 1, d)
    npost = norm_post.reshape(depth, 1, d)
    b_forget = jnp.pad(ev_b_forget, ((0, 0), (0, LANES - N_HEADS_FOX))).reshape(-1, 1, LANES)
    lam_params = (ev_lambda_q1, ev_lambda_k1, ev_lambda_q2, ev_lambda_k2, ev_subln)
    pending = None
    for layer in range(depth):
        idx = layer // 2
        if layer % 2 == 0:
            proj, fa, x = _in_projection(x, mod, layer, npre, ev_w_main, idx, ev_wf, rope, even_rope, even_dst,
                                         pending)
            aug = _forget_prefix(fa, b_forget, idx)
            lam_init = 0.8 - 0.6 * math.exp(-0.3 * layer)
            o = (_attention(proj, "fox", aug=aug),
                 _attention(proj, "diff", params=lam_params, par_idx=idx, lam_init=lam_init))
        else:
            proj, _, x = _in_projection(x, mod, layer, npre, od_w_main, idx, None, rope, odd_rope, odd_dst,
                                        pending)
            o = (_attention(proj, "odd"),)
        pending = (o, w_outs[layer % 2], idx, npost, layer)
    o, w_out, idx, _, layer = pending
    return _out_projection(o, w_out, idx, x, mod, layer, npost)
```
